```python
import jax, jax.numpy as jnp
from jax import lax
import numpy as np

D_MODEL = 1024
BATCH = 4
SEQ = 8192
DEPTH = 4

N_A = DEPTH // 2
N_B = DEPTH - N_A
N_HEADS = 16
HEAD_DIM = D_MODEL // N_HEADS
CONV_WIDTH = 3
D_FF = ((8 * D_MODEL // 3 + 255) // 256) * 256
DILATED_GROUPS = ((128, 1), (512, 4), (2048, 16))
BLOCK = 128
EPS = 1e-6

kernel_name = "yoco_shortconv_dilated_alibi_trunk"


def _rms_norm(x, g):
    xf = x.astype(jnp.float32)
    y = xf * lax.rsqrt(jnp.mean(xf * xf, axis=-1, keepdims=True) + EPS)
    return (y * g.astype(jnp.float32)).astype(x.dtype)


def _swiglu(x, w_in, w_out):
    gate, up = jnp.split(x @ w_in, 2, axis=-1)
    return (jax.nn.silu(gate) * up) @ w_out


def _short_conv(x, w_in, w_conv, w_out):
    b_gate, c_gate, h = jnp.split(x @ w_in, 3, axis=-1)
    u = c_gate * h
    u = lax.conv_general_dilated(
        u, w_conv[:, None, :].astype(u.dtype), window_strides=(1,),
        padding=[(CONV_WIDTH - 1, 0)],
        dimension_numbers=("NWC", "WIO", "NWC"),
        feature_group_count=D_MODEL)
    return (b_gate * u) @ w_out


def _alibi_slopes():
    h = np.arange(N_HEADS, dtype=np.float32) + 1.0
    return jnp.asarray(np.power(2.0, -8.0 * h / N_HEADS), dtype=jnp.float32)


def _strided_blocks(t, dilation):
    b, s, h, e = t.shape
    mult = dilation * BLOCK
    sp = -(-s // mult) * mult
    t = jnp.pad(t, ((0, 0), (0, sp - s), (0, 0), (0, 0)))
    return t.reshape(b, sp // mult, BLOCK, dilation, h, e)


def _with_prev_block(tb):
    prev = jnp.concatenate([jnp.zeros_like(tb[:, :1]), tb[:, :-1]], axis=1)
    return jnp.concatenate([prev, tb], axis=2)


def _shared_kv(h, g_kv, w_kv):
    b, s, _ = h.shape
    k, v = jnp.split(_rms_norm(h, g_kv) @ w_kv, 2, axis=-1)
    k = k.reshape(b, s, N_HEADS, HEAD_DIM)
    v = v.reshape(b, s, N_HEADS, HEAD_DIM)
    return [(_with_prev_block(_strided_blocks(k, d)), _with_prev_block(_strided_blocks(v, d)))
            for (_, d) in DILATED_GROUPS]


def _branch_attention(q, k_cat, v_cat, slopes, window, dilation):
    b, s, h, e = q.shape
    qb = _strided_blocks(q, dilation)
    nb = qb.shape[1]
    sc = jnp.einsum("bnqrhe,bnkrhe->bnrhqk", qb, k_cat).astype(jnp.float32)
    u = jnp.arange(BLOCK)[:, None]
    kk = jnp.arange(2 * BLOCK)[None, :]
    delta = u + BLOCK - kk
    band = (delta >= 0) & (delta <= window // dilation)
    first = (jnp.arange(nb) == 0)[:, None, None]
    valid = band[None] & ~(first & (kk < BLOCK)[None])
    bias = -slopes[:, None, None] * (delta * dilation).astype(jnp.float32)[None]
    sc = jnp.where(valid[:, None, None], sc + bias, -jnp.inf)
    m = jnp.max(sc, axis=-1, keepdims=True)
    p = jnp.exp(sc - m)
    den = jnp.sum(p, axis=-1, keepdims=True)
    o = jnp.einsum("bnrhqk,bnkrhe->bnqrhe", p / den, v_cat.astype(jnp.float32))
    lse = (m + jnp.log(den))[..., 0]
    o = o.reshape(b, nb * BLOCK * dilation, h, e)[:, :s]
    lse = lse.transpose(0, 1, 4, 2, 3).reshape(b, nb * BLOCK * dilation, h)[:, :s]
    return o, lse


def _dilated_attention(x, w_q, w_o, shared, slopes):
    b, s, _ = x.shape
    q = (x @ w_q).reshape(b, s, N_HEADS, HEAD_DIM) * (HEAD_DIM ** -0.5)
    outs, lses = [], []
    for (window, dil), (k_cat, v_cat) in zip(DILATED_GROUPS, shared):
        o, l = _branch_attention(q, k_cat, v_cat, slopes, window, dil)
        outs.append(o)
        lses.append(l)
    wts = jax.nn.softmax(jnp.stack(lses), axis=0)
    o = jnp.sum(wts[..., None] * jnp.stack(outs), axis=0)
    return o.reshape(b, s, D_MODEL).astype(x.dtype) @ w_o


def setup_inputs(seed: int = 0) -> dict:
    key = jax.random.key(seed)
    ks = jax.random.split(key, 12)
    f32 = jnp.float32
    nrm = lambda k, shape, fan_in: jax.random.normal(k, shape, f32) * (fan_in ** -0.5)
    return {
        "x": jax.random.normal(ks[0], (BATCH, SEQ, D_MODEL), f32),
        "norm_g": 1.0 + 0.05 * jax.random.normal(ks[1], (DEPTH, 4, D_MODEL), f32),
        "conv_in_w": nrm(ks[2], (N_A, D_MODEL, 3 * D_MODEL), D_MODEL),
        "conv_w": nrm(ks[3], (N_A, CONV_WIDTH, D_MODEL), CONV_WIDTH),
        "conv_out_w": nrm(ks[4], (N_A, D_MODEL, D_MODEL), D_MODEL),
        "kv_norm_g": 1.0 + 0.05 * jax.random.normal(ks[5], (D_MODEL,), f32),
        "kv_w": nrm(ks[6], (D_MODEL, 2 * D_MODEL), D_MODEL),
        "q_w": nrm(ks[7], (N_B, D_MODEL, D_MODEL), D_MODEL),
        "o_w": nrm(ks[8], (N_B, D_MODEL, D_MODEL), D_MODEL),
        "ffn_in_w": nrm(ks[9], (DEPTH, D_MODEL, 2 * D_FF), D_MODEL),
        "ffn_out_w": nrm(ks[10], (DEPTH, D_FF, D_MODEL), D_FF),
    }


def reference(x, norm_g, conv_in_w, conv_w, conv_out_w, kv_norm_g, kv_w, q_w, o_w,
              ffn_in_w, ffn_out_w):
    slopes = _alibi_slopes()
    shared = None
    for layer in range(DEPTH):
        g = norm_g[layer]
        xn = _rms_norm(x, g[0])
        if layer < N_A:
            mix = _short_conv(xn, conv_in_w[layer], conv_w[layer], conv_out_w[layer])
        else:
            if shared is None:
                shared = _shared_kv(x, kv_norm_g, kv_w)
            j = layer - N_A
            mix = _dilated_attention(xn, q_w[j], o_w[j], shared, slopes)
        x = x + _rms_norm(mix, g[1])
        ff = _swiglu(_rms_norm(x, g[2]), ffn_in_w[layer], ffn_out_w[layer])
        x = x + _rms_norm(ff, g[3])
    return x
```

```python
import functools

import numpy as np
import jax
import jax.numpy as jnp
from jax import lax
from jax.experimental import pallas as pl
from jax.experimental.pallas import tpu as pltpu

N_HEADS = 16
HEAD_DIM = 64
CONV_WIDTH = 3
DILATIONS = (1, 4, 16)
BLOCK = 128
EPS = 1e-6
MASKED = -1e30

LANES = 128
CARRY_ROWS = 8
VMEM_LIMIT = 56 * 1024 * 1024

F32 = jnp.float32
BF16 = jnp.bfloat16


def _rms_norm(x, g):
    return x * lax.rsqrt(jnp.mean(x * x, axis=-1, keepdims=True) + EPS) * g


def _dot(a, b):
    return jnp.dot(a, b, preferred_element_type=F32)


def _resident(shape):
    return pl.BlockSpec(shape, lambda *_: (0,) * len(shape), pipeline_mode=pl.Buffered(1))


def _params(n_axes):
    return pltpu.CompilerParams(dimension_semantics=("arbitrary",) * n_axes,
                                vmem_limit_bytes=VMEM_LIMIT)


def _conv_mixer_kernel(x_ref, g0_ref, g1_ref, win_ref, cw_ref, wout_ref, o_ref, u_ref):
    tm, d = x_ref.shape[1], x_ref.shape[2]

    @pl.when(pl.program_id(1) == 0)
    def _():
        u_ref[0:CARRY_ROWS, :] = jnp.zeros((CARRY_ROWS, d), F32)

    @pl.when(pl.program_id(1) != 0)
    def _():
        u_ref[0:CARRY_ROWS, :] = u_ref[tm:tm + CARRY_ROWS, :]

    x = x_ref[0]
    xn = _rms_norm(x, g0_ref[...]).astype(BF16)
    c_gate = _dot(xn, win_ref[:, d:2 * d])
    h = _dot(xn, win_ref[:, 2 * d:3 * d])
    u_ref[CARRY_ROWS:CARRY_ROWS + tm, :] = c_gate * h
    conv = cw_ref[CONV_WIDTH - 1:CONV_WIDTH, :] * u_ref[CARRY_ROWS:CARRY_ROWS + tm, :]
    for k in range(CONV_WIDTH - 1):
        back = CONV_WIDTH - 1 - k
        conv = conv + cw_ref[k:k + 1, :] * u_ref[CARRY_ROWS - back:CARRY_ROWS - back + tm, :]
    b_gate = _dot(xn, win_ref[:, 0:d])
    y = _dot((b_gate * conv).astype(BF16), wout_ref[...])
    o_ref[0] = x + _rms_norm(y, g1_ref[...])


def _conv_mixer(x, g0, g1, w_in, w_conv, w_out, tm):
    b, s, d = x.shape
    return pl.pallas_call(
        _conv_mixer_kernel,
        grid=(b, s // tm),
        in_specs=[
            pl.BlockSpec((1, tm, d), lambda i, j: (i, j, 0)),
            _resident((1, d)), _resident((1, d)),
            _resident((d, 3 * d)), _resident((CONV_WIDTH, d)), _resident((d, d)),
        ],
        out_specs=pl.BlockSpec((1, tm, d), lambda i, j: (i, j, 0)),
        out_shape=jax.ShapeDtypeStruct((b, s, d), F32),
        scratch_shapes=[pltpu.VMEM((tm + CARRY_ROWS, d), F32)],
        compiler_params=_params(2),
        name="conv_mixer",
    )(x, g0, g1, w_in, w_conv, w_out)


def _ffn_kernel(x_ref, g2_ref, g3_ref, win_ref, wout_ref, o_ref):
    d_ff = wout_ref.shape[0]
    x = x_ref[...]
    hn = _rms_norm(x, g2_ref[...]).astype(BF16)
    gate = _dot(hn, win_ref[:, 0:d_ff])
    up = _dot(hn, win_ref[:, d_ff:2 * d_ff])
    act = (gate * jax.nn.sigmoid(gate) * up).astype(BF16)
    ff = _dot(act, wout_ref[...])
    o_ref[...] = x + _rms_norm(ff, g3_ref[...])


def _ffn(x, g2, g3, w_in, w_out, tm):
    t, d = x.shape
    d_ff = w_out.shape[0]
    return pl.pallas_call(
        _ffn_kernel,
        grid=(t // tm,),
        in_specs=[
            pl.BlockSpec((tm, d), lambda i: (i, 0)),
            _resident((1, d)), _resident((1, d)),
            _resident((d, 2 * d_ff)), _resident((d_ff, d)),
        ],
        out_specs=pl.BlockSpec((tm, d), lambda i: (i, 0)),
        out_shape=jax.ShapeDtypeStruct((t, d), F32),
        compiler_params=_params(1),
        name="ffn",
    )(x, g2, g3, w_in, w_out)


def _qkv_kernel(x_ref, g0_ref, gkv_ref, wq_ref, wkv_ref, q_ref, k_ref, v_ref):
    d = x_ref.shape[1]
    x = x_ref[...]
    xr = x * lax.rsqrt(jnp.mean(x * x, axis=-1, keepdims=True) + EPS)
    q = _dot((xr * g0_ref[...]).astype(BF16), wq_ref[...])
    q_ref[...] = (q * (HEAD_DIM ** -0.5)).astype(BF16)
    xkv = (xr * gkv_ref[...]).astype(BF16)
    k_ref[...] = _dot(xkv, wkv_ref[:, 0:d]).astype(BF16)
    v_ref[...] = _dot(xkv, wkv_ref[:, d:2 * d]).astype(BF16)


def _q_kernel(x_ref, g0_ref, wq_ref, q_ref):
    q = _dot(_rms_norm(x_ref[...], g0_ref[...]).astype(BF16), wq_ref[...])
    q_ref[...] = (q * (HEAD_DIM ** -0.5)).astype(BF16)


def _qkv_proj(x, g0, gkv, wq, wkv, tm):
    t, d = x.shape
    tile = pl.BlockSpec((tm, d), lambda i: (i, 0))
    out = jax.ShapeDtypeStruct((t, d), BF16)
    return pl.pallas_call(
        _qkv_kernel,
        grid=(t // tm,),
        in_specs=[tile, _resident((1, d)), _resident((1, d)),
                  _resident((d, d)), _resident((d, 2 * d))],
        out_specs=[tile, tile, tile],
        out_shape=[out, out, out],
        compiler_params=_params(1),
        name="qkv_proj",
    )(x, g0, gkv, wq, wkv)


def _q_proj(x, g0, wq, tm):
    t, d = x.shape
    tile = pl.BlockSpec((tm, d), lambda i: (i, 0))
    return pl.pallas_call(
        _q_kernel,
        grid=(t // tm,),
        in_specs=[tile, _resident((1, d)), _resident((d, d))],
        out_specs=tile,
        out_shape=jax.ShapeDtypeStruct((t, d), BF16),
        compiler_params=_params(1),
        name="q_proj",
    )(x, g0, wq)


def _attn_kernel(q_ref, kp_ref, kc_ref, vp_ref, vc_ref, bias_ref, o_ref, lse_ref):
    q = q_ref[0]
    kcat = jnp.concatenate([kp_ref[0], kc_ref[0]], axis=0)
    vcat = jnp.concatenate([vp_ref[0], vc_ref[0]], axis=0)
    lane = lax.broadcasted_iota(jnp.int32, (BLOCK, LANES), 1)
    first_head_lanes = lane < HEAD_DIM
    lse_tile = jnp.zeros((BLOCK, LANES), F32)
    for pair in range(N_HEADS * HEAD_DIM // LANES):
        cols = slice(pair * LANES, (pair + 1) * LANES)
        q2, k2, v2 = q[:, cols], kcat[:, cols], vcat[:, cols]
        outs = []
        for half in range(LANES // HEAD_DIM):
            head = pair * (LANES // HEAD_DIM) + half
            own = first_head_lanes if half == 0 else ~first_head_lanes
            qh = jnp.where(own, q2, jnp.zeros_like(q2))
            sc = lax.dot_general(qh, k2, (((1,), (1,)), ((), ())), preferred_element_type=F32)
            sc = sc + bias_ref[0, head]
            m = jnp.max(sc, axis=-1, keepdims=True)
            p = jnp.exp(sc - m)
            den = jnp.sum(p, axis=-1, keepdims=True)
            outs.append(_dot(p.astype(BF16), v2) / den)
            lse_tile = jnp.where(lane == head, m + jnp.log(den), lse_tile)
        o_ref[0, :, cols] = jnp.where(first_head_lanes, outs[0], outs[1]).astype(BF16)
    lse_ref[0] = lse_tile


def _attention_bias(dilation):
    heads = np.arange(N_HEADS, dtype=np.float32) + 1.0
    slopes = np.power(2.0, -8.0 * heads / N_HEADS).astype(np.float32)
    u = np.arange(BLOCK)[:, None]
    kk = np.arange(2 * BLOCK)[None, :]
    delta = u + BLOCK - kk
    band = (delta >= 0) & (delta <= BLOCK)
    bias = -slopes[:, None, None] * (delta * dilation).astype(np.float32)[None]
    later = np.where(band[None], bias, np.float32(MASKED))
    first = np.where((band & (kk >= BLOCK))[None], bias, np.float32(MASKED))
    return jnp.asarray(np.stack([first, later]).astype(np.float32))


def _attention_branch(q, k, v, dilation):
    b, s, d = q.shape
    sub = s // dilation
    nb = sub // BLOCK
    view = lambda t: t.reshape(b, sub, dilation * d)
    own = pl.BlockSpec((1, BLOCK, d), lambda n, i, r: (i, n, r))
    prev = pl.BlockSpec((1, BLOCK, d), lambda n, i, r: (i, jnp.maximum(n - 1, 0), r))
    bias = _attention_bias(dilation)
    o, lse = pl.pallas_call(
        _attn_kernel,
        grid=(nb, b, dilation),
        in_specs=[own, prev, own, prev, own,
                  pl.BlockSpec((1, N_HEADS, BLOCK, 2 * BLOCK),
                               lambda n, i, r: (jnp.minimum(n, 1), 0, 0, 0))],
        out_specs=[own, pl.BlockSpec((1, BLOCK, LANES), lambda n, i, r: (i, n, r))],
        out_shape=[jax.ShapeDtypeStruct((b, sub, dilation * d), BF16),
                   jax.ShapeDtypeStruct((b, sub, dilation * LANES), F32)],
        compiler_params=_params(3),
        name=f"attention_d{dilation}",
    )(view(q), view(k), view(k), view(v), view(v), bias)
    return o.reshape(b * s, d), lse.reshape(b * s, LANES)


def _merge_oproj_kernel(x_ref, o1_ref, o2_ref, o3_ref, l1_ref, l2_ref, l3_ref,
                        expand_ref, wo_ref, g1_ref, out_ref):
    lses = [l1_ref[...], l2_ref[...], l3_ref[...]]
    top = jnp.maximum(jnp.maximum(lses[0], lses[1]), lses[2])
    es = [jnp.exp(l - top) for l in lses]
    total = es[0] + es[1] + es[2]
    merged = None
    for e, o_ref in zip(es, (o1_ref, o2_ref, o3_ref)):
        w = e / total
        w_hi = w.astype(BF16)
        w_lo = (w - w_hi.astype(F32)).astype(BF16)
        w_full = _dot(w_hi, expand_ref[...]) + _dot(w_lo, expand_ref[...])
        term = w_full * o_ref[...].astype(F32)
        merged = term if merged is None else merged + term
    y = _dot(merged.astype(BF16), wo_ref[...])
    out_ref[...] = x_ref[...] + _rms_norm(y, g1_ref[...])


def _merge_oproj(x, outs, lses, wo, g1, tm):
    t, d = x.shape
    expand = np.zeros((LANES, d), np.float32)
    for head in range(N_HEADS):
        expand[head, head * HEAD_DIM:(head + 1) * HEAD_DIM] = 1.0
    tile = pl.BlockSpec((tm, d), lambda i: (i, 0))
    ltile = pl.BlockSpec((tm, LANES), lambda i: (i, 0))
    return pl.pallas_call(
        _merge_oproj_kernel,
        grid=(t // tm,),
        in_specs=[tile, tile, tile, tile, ltile, ltile, ltile,
                  _resident((LANES, d)), _resident((d, d)), _resident((1, d))],
        out_specs=tile,
        out_shape=jax.ShapeDtypeStruct((t, d), F32),
        compiler_params=_params(1),
        name="merge_oproj",
    )(x, *outs, *lses, jnp.asarray(expand, BF16), wo, g1)


def kernel(x, norm_g, conv_in_w, conv_w, conv_out_w, kv_norm_g, kv_w, q_w, o_w, ffn_in_w, ffn_out_w):
    b, s, d = x.shape
    depth = norm_g.shape[0]
    n_conv = conv_in_w.shape[0]
    assert d == N_HEADS * HEAD_DIM and s % (max(DILATIONS) * BLOCK) == 0
    tm = 256
    row = lambda g: g.reshape(1, d)
    k = v = None
    for layer in range(depth):
        g = norm_g[layer]
        if layer < n_conv:
            x = _conv_mixer(x, row(g[0]), row(g[1]), conv_in_w[layer].astype(BF16), conv_w[layer],
                            conv_out_w[layer].astype(BF16), tm)
            x2 = x.reshape(b * s, d)
        else:
            j = layer - n_conv
            x2 = x.reshape(b * s, d)
            if k is None:
                q, k, v = _qkv_proj(x2, row(g[0]), row(kv_norm_g), q_w[j].astype(BF16),
                                    kv_w.astype(BF16), tm)
            else:
                q = _q_proj(x2, row(g[0]), q_w[j].astype(BF16), tm)
            shape3 = (b, s, d)
            branches = [_attention_branch(q.reshape(shape3), k.reshape(shape3), v.reshape(shape3), dil)
                        for dil in DILATIONS]
            x2 = _merge_oproj(x2, [o for o, _ in branches], [l for _, l in branches],
                              o_w[j].astype(BF16), row(g[1]), tm)
        x2 = _ffn(x2, row(g[2]), row(g[3]), ffn_in_w[layer].astype(BF16),
                  ffn_out_w[layer].astype(BF16), tm)
        x = x2.reshape(b, s, d)
    return x
```

```python
import math

import numpy as np
import jax
import jax.numpy as jnp
from jax import lax
from jax.experimental import pallas as pl
from jax.experimental.pallas import tpu as pltpu

N_HEADS = 16
HEAD_DIM = 64
CONV_WIDTH = 3
DILATIONS = (1, 4, 16)
BLOCK = 128
EPS = 1e-6
MASKED = -1e30
LOG2E = math.log2(math.e)

LANES = 128
CARRY_ROWS = 8
VMEM_LIMIT = 56 * 1024 * 1024
HEADS_PER_VREG = LANES // HEAD_DIM
N_PAIRS = N_HEADS // HEADS_PER_VREG

F32 = jnp.float32
BF16 = jnp.bfloat16


def _rms_norm(x, g):
    return x * lax.rsqrt(jnp.mean(x * x, axis=-1, keepdims=True) + EPS) * g


def _dot(a, b):
    return jnp.dot(a, b, preferred_element_type=F32)


def _resident(shape):
    return pl.BlockSpec(shape, lambda *_: (0,) * len(shape), pipeline_mode=pl.Buffered(1))


def _params(n_axes):
    return pltpu.CompilerParams(dimension_semantics=("arbitrary",) * n_axes,
                                vmem_limit_bytes=VMEM_LIMIT)


def _conv_mixer_kernel(x_ref, g0_ref, g1_ref, win_ref, cw_ref, wout_ref, o_ref, u_ref):
    tm, d = x_ref.shape[1], x_ref.shape[2]

    @pl.when(pl.program_id(1) == 0)
    def _():
        u_ref[0:CARRY_ROWS, :] = jnp.zeros((CARRY_ROWS, d), F32)

    @pl.when(pl.program_id(1) != 0)
    def _():
        u_ref[0:CARRY_ROWS, :] = u_ref[tm:tm + CARRY_ROWS, :]

    x = x_ref[0]
    xn = _rms_norm(x, g0_ref[...]).astype(BF16)
    c_gate = _dot(xn, win_ref[:, d:2 * d])
    h = _dot(xn, win_ref[:, 2 * d:3 * d])
    u_ref[CARRY_ROWS:CARRY_ROWS + tm, :] = c_gate * h
    conv = cw_ref[CONV_WIDTH - 1:CONV_WIDTH, :] * u_ref[CARRY_ROWS:CARRY_ROWS + tm, :]
    for k in range(CONV_WIDTH - 1):
        back = CONV_WIDTH - 1 - k
        conv = conv + cw_ref[k:k + 1, :] * u_ref[CARRY_ROWS - back:CARRY_ROWS - back + tm, :]
    b_gate = _dot(xn, win_ref[:, 0:d])
    y = _dot((b_gate * conv).astype(BF16), wout_ref[...])
    o_ref[0] = x + _rms_norm(y, g1_ref[...])


def _conv_mixer(x, g0, g1, w_in, w_conv, w_out, tm):
    b, s, d = x.shape
    return pl.pallas_call(
        _conv_mixer_kernel,
        grid=(b, s // tm),
        in_specs=[
            pl.BlockSpec((1, tm, d), lambda i, j: (i, j, 0)),
            _resident((1, d)), _resident((1, d)),
            _resident((d, 3 * d)), _resident((CONV_WIDTH, d)), _resident((d, d)),
        ],
        out_specs=pl.BlockSpec((1, tm, d), lambda i, j: (i, j, 0)),
        out_shape=jax.ShapeDtypeStruct((b, s, d), F32),
        scratch_shapes=[pltpu.VMEM((tm + CARRY_ROWS, d), F32)],
        compiler_params=_params(2),
        name="conv_mixer",
    )(x, g0, g1, w_in, w_conv, w_out)


def _ffn_kernel(x_ref, g2_ref, g3_ref, win_ref, wout_ref, o_ref):
    d_ff = wout_ref.shape[0]
    x = x_ref[...]
    hn = _rms_norm(x, g2_ref[...]).astype(BF16)
    gate = _dot(hn, win_ref[:, 0:d_ff])
    up = _dot(hn, win_ref[:, d_ff:2 * d_ff])
    act = (gate * jax.nn.sigmoid(gate) * up).astype(BF16)
    ff = _dot(act, wout_ref[...])
    o_ref[...] = x + _rms_norm(ff, g3_ref[...])


def _ffn(x, g2, g3, w_in, w_out, tm):
    t, d = x.shape
    d_ff = w_out.shape[0]
    return pl.pallas_call(
        _ffn_kernel,
        grid=(t // tm,),
        in_specs=[
            pl.BlockSpec((tm, d), lambda i: (i, 0)),
            _resident((1, d)), _resident((1, d)),
            _resident((d, 2 * d_ff)), _resident((d_ff, d)),
        ],
        out_specs=pl.BlockSpec((tm, d), lambda i: (i, 0)),
        out_shape=jax.ShapeDtypeStruct((t, d), F32),
        compiler_params=_params(1),
        name="ffn",
    )(x, g2, g3, w_in, w_out)


def _store_views(val, stage_ref, out_refs):
    tm, d_model = val.shape
    n_chunks = d_model // LANES
    for c in range(n_chunks):
        stage_ref[c] = val[:, c * LANES:(c + 1) * LANES]
    for dil, ref in zip(DILATIONS, out_refs):
        if dil == 1:
            ref[...] = val.astype(BF16)
            continue
        for r in range(dil):
            for c in range(n_chunks):
                rows = stage_ref[c, pl.ds(r, tm // dil, stride=dil), :]
                col = r * d_model + c * LANES
                ref[:, col:col + LANES] = rows.astype(BF16)


def _qkv_kernel(x_ref, g0_ref, gkv_ref, wq_ref, wkv_ref, *refs):
    n = len(DILATIONS)
    q_refs, k_refs, v_refs, stage_ref = refs[0:n], refs[n:2 * n], refs[2 * n:3 * n], refs[3 * n]
    d = x_ref.shape[1]
    x = x_ref[...]
    xr = x * lax.rsqrt(jnp.mean(x * x, axis=-1, keepdims=True) + EPS)
    q = _dot((xr * g0_ref[...]).astype(BF16), wq_ref[...])
    _store_views(q * (HEAD_DIM ** -0.5 * LOG2E), stage_ref, q_refs)
    xkv = (xr * gkv_ref[...]).astype(BF16)
    _store_views(_dot(xkv, wkv_ref[:, 0:d]), stage_ref, k_refs)
    _store_views(_dot(xkv, wkv_ref[:, d:2 * d]), stage_ref, v_refs)


def _q_kernel(x_ref, g0_ref, wq_ref, *refs):
    q_refs, stage_ref = refs[:-1], refs[-1]
    q = _dot(_rms_norm(x_ref[...], g0_ref[...]).astype(BF16), wq_ref[...])
    _store_views(q * (HEAD_DIM ** -0.5 * LOG2E), stage_ref, q_refs)


def _view_specs(t, d, tm):
    specs = [pl.BlockSpec((tm // dil, dil * d), lambda i: (i, 0)) for dil in DILATIONS]
    shapes = [jax.ShapeDtypeStruct((t // dil, dil * d), BF16) for dil in DILATIONS]
    return specs, shapes


def _qkv_proj(x, g0, gkv, wq, wkv, tm):
    t, d = x.shape
    specs, shapes = _view_specs(t, d, tm)
    n = len(DILATIONS)
    outs = pl.pallas_call(
        _qkv_kernel,
        grid=(t // tm,),
        in_specs=[pl.BlockSpec((tm, d), lambda i: (i, 0)), _resident((1, d)), _resident((1, d)),
                  _resident((d, d)), _resident((d, 2 * d))],
        out_specs=specs * 3,
        out_shape=shapes * 3,
        scratch_shapes=[pltpu.VMEM((d // LANES, tm, LANES), F32)],
        compiler_params=_params(1),
        name="qkv_proj",
    )(x, g0, gkv, wq, wkv)
    return outs[0:n], outs[n:2 * n], outs[2 * n:3 * n]


def _q_proj(x, g0, wq, tm):
    t, d = x.shape
    specs, shapes = _view_specs(t, d, tm)
    return pl.pallas_call(
        _q_kernel,
        grid=(t // tm,),
        in_specs=[pl.BlockSpec((tm, d), lambda i: (i, 0)), _resident((1, d)), _resident((d, d))],
        out_specs=specs,
        out_shape=shapes,
        scratch_shapes=[pltpu.VMEM((d // LANES, tm, LANES), F32)],
        compiler_params=_params(1),
        name="q_proj",
    )(x, g0, wq)


def _attn_kernel(q_ref, kp_ref, kc_ref, vp_ref, vc_ref, bias_ref, o_ref, m_ref, l_ref):
    q = q_ref[...]
    kcat = jnp.concatenate([kp_ref[...], kc_ref[...]], axis=0)
    vcat = jnp.concatenate([vp_ref[...], vc_ref[...]], axis=0)
    lane = lax.broadcasted_iota(jnp.int32, (BLOCK, LANES), 1)
    first_head_lanes = lane < HEAD_DIM
    m_tile = jnp.zeros((BLOCK, LANES), F32)
    l_tile = jnp.ones((BLOCK, LANES), F32)
    for pair in range(N_PAIRS):
        cols = slice(pair * LANES, (pair + 1) * LANES)
        q2, k2, v2 = q[:, cols], kcat[:, cols], vcat[:, cols]
        zero = jnp.zeros_like(q2)
        q_rows = jnp.concatenate([jnp.where(first_head_lanes, q2, zero),
                                  jnp.where(first_head_lanes, zero, q2)], axis=0)
        sc = lax.dot_general(q_rows, k2, (((1,), (1,)), ((), ())), preferred_element_type=F32)
        sc = sc + bias_ref[0, pair]
        m = jnp.max(sc, axis=-1, keepdims=True)
        p = jnp.exp2(sc - m)
        den = jnp.sum(p, axis=-1, keepdims=True)
        pv = _dot(p.astype(BF16), v2)
        o_ref[:, cols] = jnp.where(first_head_lanes, pv[0:BLOCK], pv[BLOCK:2 * BLOCK]).astype(BF16)
        for half in range(HEADS_PER_VREG):
            head = pair * HEADS_PER_VREG + half
            rows = slice(half * BLOCK, (half + 1) * BLOCK)
            m_tile = jnp.where(lane == head, m[rows], m_tile)
            l_tile = jnp.where(lane == head, den[rows], l_tile)
    m_ref[...] = m_tile
    l_ref[...] = l_tile


def _attention_bias(dilation):
    heads = np.arange(N_HEADS, dtype=np.float64) + 1.0
    slopes = np.power(2.0, -8.0 * heads / N_HEADS)
    u = np.arange(BLOCK)[:, None]
    kk = np.arange(2 * BLOCK)[None, :]
    delta = u + BLOCK - kk
    band = (delta >= 0) & (delta <= BLOCK)
    bias = -slopes[:, None, None] * (delta * dilation)[None] * LOG2E
    later = np.where(band[None], bias, MASKED)
    first = np.where((band & (kk >= BLOCK))[None], bias, MASKED)
    stacked = np.stack([first, later]).reshape(2, N_PAIRS, HEADS_PER_VREG * BLOCK, 2 * BLOCK)
    return jnp.asarray(stacked.astype(np.float32))


def _attention_branch(q, k, v, dilation, n_seq):
    rows, width = q.shape
    d = width // dilation
    nb = rows // n_seq // BLOCK
    own = lambda w: pl.BlockSpec((BLOCK, w), lambda n, i, r: (i * nb + n, r))
    prev = pl.BlockSpec((BLOCK, d), lambda n, i, r: (i * nb + jnp.maximum(n - 1, 0), r))
    stat = jax.ShapeDtypeStruct((rows, dilation * LANES), F32)
    return pl.pallas_call(
        _attn_kernel,
        grid=(nb, n_seq, dilation),
        in_specs=[own(d), prev, own(d), prev, own(d),
                  pl.BlockSpec((1, N_PAIRS, HEADS_PER_VREG * BLOCK, 2 * BLOCK),
                               lambda n, i, r: (jnp.minimum(n, 1), 0, 0, 0))],
        out_specs=[own(d), own(LANES), own(LANES)],
        out_shape=[jax.ShapeDtypeStruct((rows, width), BF16), stat, stat],
        compiler_params=_params(3),
        name=f"attention_d{dilation}",
    )(q, k, k, v, v, _attention_bias(dilation))


def _load_natural(view_ref, stage_ref, dil):
    if dil == 1:
        return view_ref[...].astype(F32)
    n_chunks, tm, _ = stage_ref.shape
    w = n_chunks * LANES
    for r in range(dil):
        for c in range(n_chunks):
            col = r * w + c * LANES
            stage_ref[c, pl.ds(r, tm // dil, stride=dil), :] = view_ref[:, col:col + LANES].astype(F32)
    if n_chunks == 1:
        return stage_ref[0]
    return jnp.concatenate([stage_ref[c] for c in range(n_chunks)], axis=-1)


def _merge_oproj_kernel(x_ref, *refs):
    n = len(DILATIONS)
    o_refs, m_refs, l_refs = refs[0:n], refs[n:2 * n], refs[2 * n:3 * n]
    expand_ref, wo_ref, g1_ref, out_ref = refs[3 * n:3 * n + 4]
    stages = refs[3 * n + 4:]
    o_stages, m_stages, l_stages = stages[0:n], stages[n:2 * n], stages[2 * n:3 * n]
    ms = [_load_natural(r, s, dil) for r, s, dil in zip(m_refs, m_stages, DILATIONS)]
    ls = [_load_natural(r, s, dil) for r, s, dil in zip(l_refs, l_stages, DILATIONS)]
    top = ms[0]
    for m in ms[1:]:
        top = jnp.maximum(top, m)
    scales = [jnp.exp2(m - top) for m in ms]
    total = scales[0] * ls[0]
    for a, l in zip(scales[1:], ls[1:]):
        total = total + a * l
    merged = None
    for a, o_ref, o_stage, dil in zip(scales, o_refs, o_stages, DILATIONS):
        w = a / total
        w_hi = w.astype(BF16)
        w_lo = (w - w_hi.astype(F32)).astype(BF16)
        w_full = _dot(w_hi, expand_ref[...]) + _dot(w_lo, expand_ref[...])
        term = w_full * _load_natural(o_ref, o_stage, dil)
        merged = term if merged is None else merged + term
    y = _dot(merged.astype(BF16), wo_ref[...])
    out_ref[...] = x_ref[...] + _rms_norm(y, g1_ref[...])


def _merge_oproj(x, outs, ms, ls, wo, g1, tm):
    t, d = x.shape
    expand = np.zeros((LANES, d), np.float32)
    for head in range(N_HEADS):
        expand[head, head * HEAD_DIM:(head + 1) * HEAD_DIM] = 1.0
    view = lambda w: [pl.BlockSpec((tm // dil, dil * w), lambda i: (i, 0)) for dil in DILATIONS]
    return pl.pallas_call(
        _merge_oproj_kernel,
        grid=(t // tm,),
        in_specs=[pl.BlockSpec((tm, d), lambda i: (i, 0))] + view(d) + view(LANES) + view(LANES)
                 + [_resident((LANES, d)), _resident((d, d)), _resident((1, d))],
        out_specs=pl.BlockSpec((tm, d), lambda i: (i, 0)),
        out_shape=jax.ShapeDtypeStruct((t, d), F32),
        scratch_shapes=[pltpu.VMEM((d // LANES, tm, LANES), F32)] * len(DILATIONS)
                       + [pltpu.VMEM((1, tm, LANES), F32)] * (2 * len(DILATIONS)),
        compiler_params=_params(1),
        name="merge_oproj",
    )(x, *outs, *ms, *ls, jnp.asarray(expand, BF16), wo, g1)


def kernel(x, norm_g, conv_in_w, conv_w, conv_out_w, kv_norm_g, kv_w, q_w, o_w, ffn_in_w, ffn_out_w):
    b, s, d = x.shape
    depth = norm_g.shape[0]
    n_conv = conv_in_w.shape[0]
    assert d == N_HEADS * HEAD_DIM and s % (max(DILATIONS) * BLOCK) == 0
    tm = 256
    row = lambda g: g.reshape(1, d)
    k_views = v_views = None
    for layer in range(depth):
        g = norm_g[layer]
        if layer < n_conv:
            x = _conv_mixer(x, row(g[0]), row(g[1]), conv_in_w[layer].astype(BF16), conv_w[layer],
                            conv_out_w[layer].astype(BF16), tm)
            x2 = x.reshape(b * s, d)
        else:
            j = layer - n_conv
            x2 = x.reshape(b * s, d)
            if k_views is None:
                q_views, k_views, v_views = _qkv_proj(x2, row(g[0]), row(kv_norm_g),
                                                      q_w[j].astype(BF16), kv_w.astype(BF16), tm)
            else:
                q_views = _q_proj(x2, row(g[0]), q_w[j].astype(BF16), tm)
            branches = [_attention_branch(q, k, v, dil, b)
                        for q, k, v, dil in zip(q_views, k_views, v_views, DILATIONS)]
            x2 = _merge_oproj(x2, [o for o, _, _ in branches], [m for _, m, _ in branches],
                              [l for _, _, l in branches], o_w[j].astype(BF16), row(g[1]), tm)
        x2 = _ffn(x2, row(g[2]), row(g[3]), ffn_in_w[layer].astype(BF16),
                  ffn_out_w[layer].astype(BF16), tm)
        x = x2.reshape(b, s, d)
    return x
```

```python
import math

import numpy as np
import jax
import jax.numpy as jnp
from jax import lax
from jax.experimental import pallas as pl
from jax.experimental.pallas import tpu as pltpu

N_HEADS = 16
HEAD_DIM = 64
CONV_WIDTH = 3
DILATIONS = (1, 4, 16)
BLOCK = 128
EPS = 1e-6
MASKED = -1e30
LOG2E = math.log2(math.e)

LANES = 128
MXU_WIDTH = 256
ATTN_TILE = 4 * BLOCK

assert DILATIONS[0] == 1
CARRY_ROWS = 8
VMEM_LIMIT = 56 * 1024 * 1024
HEADS_PER_VREG = LANES // HEAD_DIM
N_PAIRS = N_HEADS // HEADS_PER_VREG

F32 = jnp.float32
BF16 = jnp.bfloat16


def _rms_norm(x, g):
    return x * lax.rsqrt(jnp.mean(x * x, axis=-1, keepdims=True) + EPS) * g


def _dot(a, b):
    return jnp.dot(a, b, preferred_element_type=F32)


def _resident(shape):
    return pl.BlockSpec(shape, lambda *_: (0,) * len(shape), pipeline_mode=pl.Buffered(1))


def _params(n_axes):
    return pltpu.CompilerParams(dimension_semantics=("arbitrary",) * n_axes,
                                vmem_limit_bytes=VMEM_LIMIT)


def _conv_mixer_kernel(x_ref, g0_ref, g1_ref, win_ref, cw_ref, wout_ref, o_ref, u_ref):
    tm, d = x_ref.shape[1], x_ref.shape[2]

    @pl.when(pl.program_id(1) == 0)
    def _():
        u_ref[0:CARRY_ROWS, :] = jnp.zeros((CARRY_ROWS, d), F32)

    @pl.when(pl.program_id(1) != 0)
    def _():
        u_ref[0:CARRY_ROWS, :] = u_ref[tm:tm + CARRY_ROWS, :]

    x = x_ref[0]
    xn = _rms_norm(x, g0_ref[...]).astype(BF16)
    c_gate = _dot(xn, win_ref[:, d:2 * d])
    h = _dot(xn, win_ref[:, 2 * d:3 * d])
    u_ref[CARRY_ROWS:CARRY_ROWS + tm, :] = c_gate * h
    conv = cw_ref[CONV_WIDTH - 1:CONV_WIDTH, :] * u_ref[CARRY_ROWS:CARRY_ROWS + tm, :]
    for k in range(CONV_WIDTH - 1):
        back = CONV_WIDTH - 1 - k
        conv = conv + cw_ref[k:k + 1, :] * u_ref[CARRY_ROWS - back:CARRY_ROWS - back + tm, :]
    b_gate = _dot(xn, win_ref[:, 0:d])
    y = _dot((b_gate * conv).astype(BF16), wout_ref[...])
    o_ref[0] = x + _rms_norm(y, g1_ref[...])


def _conv_mixer(x, g0, g1, w_in, w_conv, w_out, tm):
    b, s, d = x.shape
    return pl.pallas_call(
        _conv_mixer_kernel,
        grid=(b, s // tm),
        in_specs=[
            pl.BlockSpec((1, tm, d), lambda i, j: (i, j, 0)),
            _resident((1, d)), _resident((1, d)),
            _resident((d, 3 * d)), _resident((CONV_WIDTH, d)), _resident((d, d)),
        ],
        out_specs=pl.BlockSpec((1, tm, d), lambda i, j: (i, j, 0)),
        out_shape=jax.ShapeDtypeStruct((b, s, d), F32),
        scratch_shapes=[pltpu.VMEM((tm + CARRY_ROWS, d), F32)],
        compiler_params=_params(2),
        name="conv_mixer",
    )(x, g0, g1, w_in, w_conv, w_out)


def _ffn_kernel(x_ref, g2_ref, g3_ref, win_ref, wout_ref, o_ref):
    d_ff = wout_ref.shape[0]
    x = x_ref[...]
    hn = _rms_norm(x, g2_ref[...]).astype(BF16)
    gate = _dot(hn, win_ref[:, 0:d_ff])
    up = _dot(hn, win_ref[:, d_ff:2 * d_ff])
    act = (gate * jax.nn.sigmoid(gate) * up).astype(BF16)
    ff = _dot(act, wout_ref[...])
    o_ref[...] = x + _rms_norm(ff, g3_ref[...])


def _ffn(x, g2, g3, w_in, w_out, tm):
    t, d = x.shape
    d_ff = w_out.shape[0]
    return pl.pallas_call(
        _ffn_kernel,
        grid=(t // tm,),
        in_specs=[
            pl.BlockSpec((tm, d), lambda i: (i, 0)),
            _resident((1, d)), _resident((1, d)),
            _resident((d, 2 * d_ff)), _resident((d_ff, d)),
        ],
        out_specs=pl.BlockSpec((tm, d), lambda i: (i, 0)),
        out_shape=jax.ShapeDtypeStruct((t, d), F32),
        compiler_params=_params(1),
        name="ffn",
    )(x, g2, g3, w_in, w_out)


def _project_to_views(xn, w_ref, col0, scale, stage_ref, out_refs):
    n_chunks, tm, _ = stage_ref.shape
    d_model = n_chunks * LANES
    for n0 in range(0, d_model, MXU_WIDTH):
        val = _dot(xn, w_ref[:, col0 + n0:col0 + n0 + MXU_WIDTH])
        if scale is not None:
            val = val * scale
        out_refs[0][:, n0:n0 + MXU_WIDTH] = val.astype(BF16)
        for c in range(MXU_WIDTH // LANES):
            stage_ref[n0 // LANES + c] = val[:, c * LANES:(c + 1) * LANES]
    for dil, ref in zip(DILATIONS[1:], out_refs[1:]):
        for r in range(dil):
            for c in range(n_chunks):
                rows = stage_ref[c, pl.ds(r, tm // dil, stride=dil), :]
                col = r * d_model + c * LANES
                ref[:, col:col + LANES] = rows.astype(BF16)


Q_SCALE = HEAD_DIM ** -0.5 * LOG2E


def _qkv_kernel(x_ref, g0_ref, gkv_ref, wq_ref, wkv_ref, *refs):
    n = len(DILATIONS)
    q_refs, k_refs, v_refs = refs[0:n], refs[n:2 * n], refs[2 * n:3 * n]
    q_stage, k_stage, v_stage = refs[3 * n:]
    d = x_ref.shape[1]
    x = x_ref[...]
    xr = x * lax.rsqrt(jnp.mean(x * x, axis=-1, keepdims=True) + EPS)
    _project_to_views((xr * g0_ref[...]).astype(BF16), wq_ref, 0, Q_SCALE, q_stage, q_refs)
    xkv = (xr * gkv_ref[...]).astype(BF16)
    _project_to_views(xkv, wkv_ref, 0, None, k_stage, k_refs)
    _project_to_views(xkv, wkv_ref, d, None, v_stage, v_refs)


def _q_kernel(x_ref, g0_ref, wq_ref, *refs):
    q_refs, stage_ref = refs[:-1], refs[-1]
    xn = _rms_norm(x_ref[...], g0_ref[...]).astype(BF16)
    _project_to_views(xn, wq_ref, 0, Q_SCALE, stage_ref, q_refs)


def _view_specs(t, d, tm):
    specs = [pl.BlockSpec((tm // dil, dil * d), lambda i: (i, 0)) for dil in DILATIONS]
    shapes = [jax.ShapeDtypeStruct((t // dil, dil * d), BF16) for dil in DILATIONS]
    return specs, shapes


def _qkv_proj(x, g0, gkv, wq, wkv, tm):
    t, d = x.shape
    specs, shapes = _view_specs(t, d, tm)
    n = len(DILATIONS)
    outs = pl.pallas_call(
        _qkv_kernel,
        grid=(t // tm,),
        in_specs=[pl.BlockSpec((tm, d), lambda i: (i, 0)), _resident((1, d)), _resident((1, d)),
                  _resident((d, d)), _resident((d, 2 * d))],
        out_specs=specs * 3,
        out_shape=shapes * 3,
        scratch_shapes=[pltpu.VMEM((d // LANES, tm, LANES), F32)] * 3,
        compiler_params=_params(1),
        name="qkv_proj",
    )(x, g0, gkv, wq, wkv)
    return outs[0:n], outs[n:2 * n], outs[2 * n:3 * n]


def _q_proj(x, g0, wq, tm):
    t, d = x.shape
    specs, shapes = _view_specs(t, d, tm)
    return pl.pallas_call(
        _q_kernel,
        grid=(t // tm,),
        in_specs=[pl.BlockSpec((tm, d), lambda i: (i, 0)), _resident((1, d)), _resident((d, d))],
        out_specs=specs,
        out_shape=shapes,
        scratch_shapes=[pltpu.VMEM((d // LANES, tm, LANES), F32)],
        compiler_params=_params(1),
        name="q_proj",
    )(x, g0, wq)


def _attn_kernel(q_ref, kp_ref, kc_ref, vp_ref, vc_ref, bias_ref, o_ref, m_ref, l_ref):
    lane = lax.broadcasted_iota(jnp.int32, (BLOCK, LANES), 1)
    first_head_lanes = lane < HEAD_DIM
    sequence_start_table = jnp.minimum(pl.program_id(0), 1)
    for sb in range(q_ref.shape[0] // BLOCK):
        rows = slice(sb * BLOCK, (sb + 1) * BLOCK)
        q = q_ref[rows, :]
        if sb == 0:
            kcat = jnp.concatenate([kp_ref[...], kc_ref[rows, :]], axis=0)
            vcat = jnp.concatenate([vp_ref[...], vc_ref[rows, :]], axis=0)
            table = sequence_start_table
        else:
            kcat = kc_ref[(sb - 1) * BLOCK:(sb + 1) * BLOCK, :]
            vcat = vc_ref[(sb - 1) * BLOCK:(sb + 1) * BLOCK, :]
            table = 1
        m_tile = jnp.zeros((BLOCK, LANES), F32)
        l_tile = jnp.ones((BLOCK, LANES), F32)
        for pair in range(N_PAIRS):
            cols = slice(pair * LANES, (pair + 1) * LANES)
            q2, k2, v2 = q[:, cols], kcat[:, cols], vcat[:, cols]
            zero = jnp.zeros_like(q2)
            q_rows = jnp.concatenate([jnp.where(first_head_lanes, q2, zero),
                                      jnp.where(first_head_lanes, zero, q2)], axis=0)
            sc = lax.dot_general(q_rows, k2, (((1,), (1,)), ((), ())), preferred_element_type=F32)
            sc = sc + bias_ref[table, pair]
            m = jnp.max(sc, axis=-1, keepdims=True)
            p = jnp.exp2(sc - m)
            den = jnp.sum(p, axis=-1, keepdims=True)
            pv = _dot(p.astype(BF16), v2)
            o_ref[rows, cols] = jnp.where(first_head_lanes, pv[0:BLOCK],
                                          pv[BLOCK:2 * BLOCK]).astype(BF16)
            for half in range(HEADS_PER_VREG):
                head = pair * HEADS_PER_VREG + half
                half_rows = slice(half * BLOCK, (half + 1) * BLOCK)
                m_tile = jnp.where(lane == head, m[half_rows], m_tile)
                l_tile = jnp.where(lane == head, den[half_rows], l_tile)
        m_ref[rows, :] = m_tile
        l_ref[rows, :] = l_tile


def _attention_bias(dilation):
    heads = np.arange(N_HEADS, dtype=np.float64) + 1.0
    slopes = np.power(2.0, -8.0 * heads / N_HEADS)
    u = np.arange(BLOCK)[:, None]
    kk = np.arange(2 * BLOCK)[None, :]
    delta = u + BLOCK - kk
    band = (delta >= 0) & (delta <= BLOCK)
    bias = -slopes[:, None, None] * (delta * dilation)[None] * LOG2E
    later = np.where(band[None], bias, MASKED)
    first = np.where((band & (kk >= BLOCK))[None], bias, MASKED)
    stacked = np.stack([first, later]).reshape(2, N_PAIRS, HEADS_PER_VREG * BLOCK, 2 * BLOCK)
    return jnp.asarray(stacked.astype(np.float32))


def _attention_branch(q, k, v, dilation, n_seq):
    rows, width = q.shape
    d = width // dilation
    tq = ATTN_TILE
    per_tile = tq // BLOCK
    nt = rows // n_seq // tq
    own = lambda w: pl.BlockSpec((tq, w), lambda n, i, r: (i * nt + n, r))
    prev = pl.BlockSpec((BLOCK, d),
                        lambda n, i, r: (jnp.maximum((i * nt + n) * per_tile - 1, 0), r))
    stat = jax.ShapeDtypeStruct((rows, dilation * LANES), F32)
    return pl.pallas_call(
        _attn_kernel,
        grid=(nt, n_seq, dilation),
        in_specs=[own(d), prev, own(d), prev, own(d),
                  _resident((2, N_PAIRS, HEADS_PER_VREG * BLOCK, 2 * BLOCK))],
        out_specs=[own(d), own(LANES), own(LANES)],
        out_shape=[jax.ShapeDtypeStruct((rows, width), BF16), stat, stat],
        compiler_params=_params(3),
        name=f"attention_d{dilation}",
    )(q, k, k, v, v, _attention_bias(dilation))


def _load_natural(view_ref, stage_ref, dil):
    if dil == 1:
        return view_ref[...].astype(F32)
    n_chunks, tm, _ = stage_ref.shape
    w = n_chunks * LANES
    for r in range(dil):
        for c in range(n_chunks):
            col = r * w + c * LANES
            stage_ref[c, pl.ds(r, tm // dil, stride=dil), :] = view_ref[:, col:col + LANES].astype(F32)
    if n_chunks == 1:
        return stage_ref[0]
    return jnp.concatenate([stage_ref[c] for c in range(n_chunks)], axis=-1)


def _merge_oproj_kernel(x_ref, *refs):
    n = len(DILATIONS)
    o_refs, m_refs, l_refs = refs[0:n], refs[n:2 * n], refs[2 * n:3 * n]
    expand_ref, wo_ref, g1_ref, out_ref = refs[3 * n:3 * n + 4]
    stages = refs[3 * n + 4:]
    o_stages, m_stages, l_stages = stages[0:n], stages[n:2 * n], stages[2 * n:3 * n]
    ms = [_load_natural(r, s, dil) for r, s, dil in zip(m_refs, m_stages, DILATIONS)]
    ls = [_load_natural(r, s, dil) for r, s, dil in zip(l_refs, l_stages, DILATIONS)]
    top = ms[0]
    for m in ms[1:]:
        top = jnp.maximum(top, m)
    scales = [jnp.exp2(m - top) for m in ms]
    total = scales[0] * ls[0]
    for a, l in zip(scales[1:], ls[1:]):
        total = total + a * l
    merged = None
    for a, o_ref, o_stage, dil in zip(scales, o_refs, o_stages, DILATIONS):
        w = a / total
        w_hi = w.astype(BF16)
        w_lo = (w - w_hi.astype(F32)).astype(BF16)
        w_full = _dot(jnp.concatenate([w_hi, w_lo], axis=-1), expand_ref[...])
        term = w_full * _load_natural(o_ref, o_stage, dil)
        merged = term if merged is None else merged + term
    y = _dot(merged.astype(BF16), wo_ref[...])
    out_ref[...] = x_ref[...] + _rms_norm(y, g1_ref[...])


def _merge_oproj(x, outs, ms, ls, wo, g1, tm):
    t, d = x.shape
    expand = np.zeros((2 * LANES, d), np.float32)
    for head in range(N_HEADS):
        expand[head, head * HEAD_DIM:(head + 1) * HEAD_DIM] = 1.0
        expand[LANES + head, head * HEAD_DIM:(head + 1) * HEAD_DIM] = 1.0
    view = lambda w: [pl.BlockSpec((tm // dil, dil * w), lambda i: (i, 0)) for dil in DILATIONS]
    return pl.pallas_call(
        _merge_oproj_kernel,
        grid=(t // tm,),
        in_specs=[pl.BlockSpec((tm, d), lambda i: (i, 0))] + view(d) + view(LANES) + view(LANES)
                 + [_resident((2 * LANES, d)), _resident((d, d)), _resident((1, d))],
        out_specs=pl.BlockSpec((tm, d), lambda i: (i, 0)),
        out_shape=jax.ShapeDtypeStruct((t, d), F32),
        scratch_shapes=[pltpu.VMEM((d // LANES, tm, LANES), F32)] * len(DILATIONS)
                       + [pltpu.VMEM((1, tm, LANES), F32)] * (2 * len(DILATIONS)),
        compiler_params=_params(1),
        name="merge_oproj",
    )(x, *outs, *ms, *ls, jnp.asarray(expand, BF16), wo, g1)


def kernel(x, norm_g, conv_in_w, conv_w, conv_out_w, kv_norm_g, kv_w, q_w, o_w, ffn_in_w, ffn_out_w):
    b, s, d = x.shape
    depth = norm_g.shape[0]
    n_conv = conv_in_w.shape[0]
    assert d == N_HEADS * HEAD_DIM and s % (max(DILATIONS) * ATTN_TILE) == 0
    tm = 256
    row = lambda g: g.reshape(1, d)
    k_views = v_views = None
    for layer in range(depth):
        g = norm_g[layer]
        if layer < n_conv:
            x = _conv_mixer(x, row(g[0]), row(g[1]), conv_in_w[layer].astype(BF16), conv_w[layer],
                            conv_out_w[layer].astype(BF16), tm)
            x2 = x.reshape(b * s, d)
        else:
            j = layer - n_conv
            x2 = x.reshape(b * s, d)
            if k_views is None:
                q_views, k_views, v_views = _qkv_proj(x2, row(g[0]), row(kv_norm_g),
                                                      q_w[j].astype(BF16), kv_w.astype(BF16), tm)
            else:
                q_views = _q_proj(x2, row(g[0]), q_w[j].astype(BF16), tm)
            branches = [_attention_branch(q, k, v, dil, b)
                        for q, k, v, dil in zip(q_views, k_views, v_views, DILATIONS)]
            x2 = _merge_oproj(x2, [o for o, _, _ in branches], [m for _, m, _ in branches],
                              [l for _, _, l in branches], o_w[j].astype(BF16), row(g[1]), tm)
        x2 = _ffn(x2, row(g[2]), row(g[3]), ffn_in_w[layer].astype(BF16),
                  ffn_out_w[layer].astype(BF16), tm)
        x = x2.reshape(b, s, d)
    return x
```

```python
import math

import numpy as np
import jax
import jax.numpy as jnp
from jax import lax
from jax.experimental import pallas as pl
from jax.experimental.pallas import tpu as pltpu

N_HEADS = 16
HEAD_DIM = 64
CONV_WIDTH = 3
DILATIONS = (1, 4, 16)
BLOCK = 128
EPS = 1e-6
MASKED = -1e30
LOG2E = math.log2(math.e)

LANES = 128
MXU_WIDTH = 256
ATTN_TILE = 4 * BLOCK
PROJ_TILE = 512
DENSE_TILE = 512

assert DILATIONS[0] == 1
CARRY_ROWS = 8
VMEM_LIMIT = 56 * 1024 * 1024
HEADS_PER_VREG = LANES // HEAD_DIM
N_PAIRS = N_HEADS // HEADS_PER_VREG

F32 = jnp.float32
BF16 = jnp.bfloat16


def _rms_norm(x, g):
    return x * lax.rsqrt(jnp.mean(x * x, axis=-1, keepdims=True) + EPS) * g


def _dot(a, b):
    return jnp.dot(a, b, preferred_element_type=F32)


def _resident(shape):
    return pl.BlockSpec(shape, lambda *_: (0,) * len(shape), pipeline_mode=pl.Buffered(1))


def _params(n_axes):
    return pltpu.CompilerParams(dimension_semantics=("arbitrary",) * n_axes,
                                vmem_limit_bytes=VMEM_LIMIT)


def _conv_mixer_kernel(x_ref, g0_ref, g1_ref, win_ref, cw_ref, wout_ref, o_ref, u_ref):
    tm, d = x_ref.shape[1], x_ref.shape[2]

    @pl.when(pl.program_id(1) == 0)
    def _():
        u_ref[0:CARRY_ROWS, :] = jnp.zeros((CARRY_ROWS, d), F32)

    @pl.when(pl.program_id(1) != 0)
    def _():
        u_ref[0:CARRY_ROWS, :] = u_ref[tm:tm + CARRY_ROWS, :]

    x = x_ref[0]
    xn = _rms_norm(x, g0_ref[...]).astype(BF16)
    c_gate = _dot(xn, win_ref[:, d:2 * d])
    h = _dot(xn, win_ref[:, 2 * d:3 * d])
    u_ref[CARRY_ROWS:CARRY_ROWS + tm, :] = c_gate * h
    conv = cw_ref[CONV_WIDTH - 1:CONV_WIDTH, :] * u_ref[CARRY_ROWS:CARRY_ROWS + tm, :]
    for k in range(CONV_WIDTH - 1):
        back = CONV_WIDTH - 1 - k
        conv = conv + cw_ref[k:k + 1, :] * u_ref[CARRY_ROWS - back:CARRY_ROWS - back + tm, :]
    b_gate = _dot(xn, win_ref[:, 0:d])
    y = _dot((b_gate * conv).astype(BF16), wout_ref[...])
    o_ref[0] = x + _rms_norm(y, g1_ref[...])


def _conv_mixer(x, g0, g1, w_in, w_conv, w_out, tm):
    b, s, d = x.shape
    return pl.pallas_call(
        _conv_mixer_kernel,
        grid=(b, s // tm),
        in_specs=[
            pl.BlockSpec((1, tm, d), lambda i, j: (i, j, 0)),
            _resident((1, d)), _resident((1, d)),
            _resident((d, 3 * d)), _resident((CONV_WIDTH, d)), _resident((d, d)),
        ],
        out_specs=pl.BlockSpec((1, tm, d), lambda i, j: (i, j, 0)),
        out_shape=jax.ShapeDtypeStruct((b, s, d), F32),
        scratch_shapes=[pltpu.VMEM((tm + CARRY_ROWS, d), F32)],
        compiler_params=_params(2),
        name="conv_mixer",
    )(x, g0, g1, w_in, w_conv, w_out)


def _ffn_kernel(x_ref, g2_ref, g3_ref, win_ref, wout_ref, o_ref):
    d_ff = wout_ref.shape[0]
    x = x_ref[...]
    hn = _rms_norm(x, g2_ref[...]).astype(BF16)
    gate = _dot(hn, win_ref[:, 0:d_ff])
    up = _dot(hn, win_ref[:, d_ff:2 * d_ff])
    act = (gate * jax.nn.sigmoid(gate) * up).astype(BF16)
    ff = _dot(act, wout_ref[...])
    o_ref[...] = x + _rms_norm(ff, g3_ref[...])


def _ffn(x, g2, g3, w_in, w_out, tm):
    t, d = x.shape
    d_ff = w_out.shape[0]
    return pl.pallas_call(
        _ffn_kernel,
        grid=(t // tm,),
        in_specs=[
            pl.BlockSpec((tm, d), lambda i: (i, 0)),
            _resident((1, d)), _resident((1, d)),
            _resident((d, 2 * d_ff)), _resident((d_ff, d)),
        ],
        out_specs=pl.BlockSpec((tm, d), lambda i: (i, 0)),
        out_shape=jax.ShapeDtypeStruct((t, d), F32),
        compiler_params=_params(1),
        name="ffn",
    )(x, g2, g3, w_in, w_out)


def _project_to_views(xn, w_ref, col0, scale, stage_ref, out_refs):
    n_chunks, tm, _ = stage_ref.shape
    d_model = n_chunks * LANES
    for n0 in range(0, d_model, MXU_WIDTH):
        val = _dot(xn, w_ref[:, col0 + n0:col0 + n0 + MXU_WIDTH])
        if scale is not None:
            val = val * scale
        out_refs[0][:, n0:n0 + MXU_WIDTH] = val.astype(BF16)
        chunks = range(n0 // LANES, (n0 + MXU_WIDTH) // LANES)
        for c in chunks:
            stage_ref[c] = val[:, c * LANES - n0:(c + 1) * LANES - n0]
        for dil, ref in zip(DILATIONS[1:], out_refs[1:]):
            for r in range(dil):
                for c in chunks:
                    rows = stage_ref[c, pl.ds(r, tm // dil, stride=dil), :]
                    col = r * d_model + c * LANES
                    ref[:, col:col + LANES] = rows.astype(BF16)


Q_SCALE = HEAD_DIM ** -0.5 * LOG2E


def _qkv_kernel(x_ref, g0_ref, gkv_ref, wq_ref, wkv_ref, *refs):
    n = len(DILATIONS)
    q_refs, k_refs, v_refs = refs[0:n], refs[n:2 * n], refs[2 * n:3 * n]
    q_stage, k_stage, v_stage = refs[3 * n:]
    d = x_ref.shape[1]
    x = x_ref[...]
    xr = x * lax.rsqrt(jnp.mean(x * x, axis=-1, keepdims=True) + EPS)
    _project_to_views((xr * g0_ref[...]).astype(BF16), wq_ref, 0, Q_SCALE, q_stage, q_refs)
    xkv = (xr * gkv_ref[...]).astype(BF16)
    _project_to_views(xkv, wkv_ref, 0, None, k_stage, k_refs)
    _project_to_views(xkv, wkv_ref, d, None, v_stage, v_refs)


def _q_kernel(x_ref, g0_ref, wq_ref, *refs):
    q_refs, stage_ref = refs[:-1], refs[-1]
    xn = _rms_norm(x_ref[...], g0_ref[...]).astype(BF16)
    _project_to_views(xn, wq_ref, 0, Q_SCALE, stage_ref, q_refs)


def _view_specs(t, d, tm):
    specs = [pl.BlockSpec((tm // dil, dil * d), lambda i: (i, 0)) for dil in DILATIONS]
    shapes = [jax.ShapeDtypeStruct((t // dil, dil * d), BF16) for dil in DILATIONS]
    return specs, shapes


def _qkv_proj(x, g0, gkv, wq, wkv, tm):
    t, d = x.shape
    specs, shapes = _view_specs(t, d, tm)
    n = len(DILATIONS)
    outs = pl.pallas_call(
        _qkv_kernel,
        grid=(t // tm,),
        in_specs=[pl.BlockSpec((tm, d), lambda i: (i, 0)), _resident((1, d)), _resident((1, d)),
                  _resident((d, d)), _resident((d, 2 * d))],
        out_specs=specs * 3,
        out_shape=shapes * 3,
        scratch_shapes=[pltpu.VMEM((d // LANES, tm, LANES), F32)] * 3,
        compiler_params=_params(1),
        name="qkv_proj",
    )(x, g0, gkv, wq, wkv)
    return outs[0:n], outs[n:2 * n], outs[2 * n:3 * n]


def _q_proj(x, g0, wq, tm):
    t, d = x.shape
    specs, shapes = _view_specs(t, d, tm)
    return pl.pallas_call(
        _q_kernel,
        grid=(t // tm,),
        in_specs=[pl.BlockSpec((tm, d), lambda i: (i, 0)), _resident((1, d)), _resident((d, d))],
        out_specs=specs,
        out_shape=shapes,
        scratch_shapes=[pltpu.VMEM((d // LANES, tm, LANES), F32)],
        compiler_params=_params(1),
        name="q_proj",
    )(x, g0, wq)


def _attn_kernel(q_ref, kp_ref, kc_ref, vp_ref, vc_ref, bias_ref, o_ref, m_ref, l_ref):
    lane = lax.broadcasted_iota(jnp.int32, (BLOCK, LANES), 1)
    first_head_lanes = lane < HEAD_DIM
    sequence_start_table = jnp.minimum(pl.program_id(0), 1)
    for sb in range(q_ref.shape[0] // BLOCK):
        rows = slice(sb * BLOCK, (sb + 1) * BLOCK)
        q = q_ref[rows, :]
        if sb == 0:
            kcat = jnp.concatenate([kp_ref[...], kc_ref[rows, :]], axis=0)
            vcat = jnp.concatenate([vp_ref[...], vc_ref[rows, :]], axis=0)
            table = sequence_start_table
        else:
            kcat = kc_ref[(sb - 1) * BLOCK:(sb + 1) * BLOCK, :]
            vcat = vc_ref[(sb - 1) * BLOCK:(sb + 1) * BLOCK, :]
            table = 1
        m_tile = jnp.zeros((BLOCK, LANES), F32)
        l_tile = jnp.ones((BLOCK, LANES), F32)
        for pair in range(N_PAIRS):
            cols = slice(pair * LANES, (pair + 1) * LANES)
            q2, k2, v2 = q[:, cols], kcat[:, cols], vcat[:, cols]
            zero = jnp.zeros_like(q2)
            q_rows = jnp.concatenate([jnp.where(first_head_lanes, q2, zero),
                                      jnp.where(first_head_lanes, zero, q2)], axis=0)
            sc = lax.dot_general(q_rows, k2, (((1,), (1,)), ((), ())), preferred_element_type=F32)
            sc = sc + bias_ref[table, pair]
            m = jnp.max(sc, axis=-1, keepdims=True)
            p = jnp.exp2(sc - m)
            den = jnp.sum(p, axis=-1, keepdims=True)
            pv = _dot(p.astype(BF16), v2)
            o_ref[rows, cols] = jnp.where(first_head_lanes, pv[0:BLOCK],
                                          pv[BLOCK:2 * BLOCK]).astype(BF16)
            for half in range(HEADS_PER_VREG):
                head = pair * HEADS_PER_VREG + half
                half_rows = slice(half * BLOCK, (half + 1) * BLOCK)
                m_tile = jnp.where(lane == head, m[half_rows], m_tile)
                l_tile = jnp.where(lane == head, den[half_rows], l_tile)
        m_ref[rows, :] = m_tile
        l_ref[rows, :] = l_tile


def _attention_bias(dilation):
    heads = np.arange(N_HEADS, dtype=np.float64) + 1.0
    slopes = np.power(2.0, -8.0 * heads / N_HEADS)
    u = np.arange(BLOCK)[:, None]
    kk = np.arange(2 * BLOCK)[None, :]
    delta = u + BLOCK - kk
    band = (delta >= 0) & (delta <= BLOCK)
    bias = -slopes[:, None, None] * (delta * dilation)[None] * LOG2E
    later = np.where(band[None], bias, MASKED)
    first = np.where((band & (kk >= BLOCK))[None], bias, MASKED)
    stacked = np.stack([first, later]).reshape(2, N_PAIRS, HEADS_PER_VREG * BLOCK, 2 * BLOCK)
    return jnp.asarray(stacked.astype(np.float32))


def _attention_branch(q, k, v, dilation, n_seq):
    rows, width = q.shape
    d = width // dilation
    tq = ATTN_TILE
    per_tile = tq // BLOCK
    nt = rows // n_seq // tq
    own = lambda w: pl.BlockSpec((tq, w), lambda n, i, r: (i * nt + n, r))
    prev = pl.BlockSpec((BLOCK, d),
                        lambda n, i, r: (jnp.maximum((i * nt + n) * per_tile - 1, 0), r))
    stat = jax.ShapeDtypeStruct((rows, dilation * LANES), F32)
    return pl.pallas_call(
        _attn_kernel,
        grid=(nt, n_seq, dilation),
        in_specs=[own(d), prev, own(d), prev, own(d),
                  _resident((2, N_PAIRS, HEADS_PER_VREG * BLOCK, 2 * BLOCK))],
        out_specs=[own(d), own(LANES), own(LANES)],
        out_shape=[jax.ShapeDtypeStruct((rows, width), BF16), stat, stat],
        compiler_params=_params(3),
        name=f"attention_d{dilation}",
    )(q, k, k, v, v, _attention_bias(dilation))


def _load_natural(view_ref, stage_ref, dil):
    if dil == 1:
        return view_ref[...].astype(F32)
    n_chunks, tm, _ = stage_ref.shape
    w = n_chunks * LANES
    for r in range(dil):
        for c in range(n_chunks):
            col = r * w + c * LANES
            stage_ref[c, pl.ds(r, tm // dil, stride=dil), :] = view_ref[:, col:col + LANES].astype(F32)
    if n_chunks == 1:
        return stage_ref[0]
    return jnp.concatenate([stage_ref[c] for c in range(n_chunks)], axis=-1)


def _merge_oproj_kernel(x_ref, *refs):
    n = len(DILATIONS)
    o_refs, m_refs, l_refs = refs[0:n], refs[n:2 * n], refs[2 * n:3 * n]
    expand_ref, wo_ref, g1_ref, out_ref = refs[3 * n:3 * n + 4]
    stages = refs[3 * n + 4:]
    o_stages, m_stages, l_stages = stages[0:n], stages[n:2 * n], stages[2 * n:3 * n]
    ms = [_load_natural(r, s, dil) for r, s, dil in zip(m_refs, m_stages, DILATIONS)]
    ls = [_load_natural(r, s, dil) for r, s, dil in zip(l_refs, l_stages, DILATIONS)]
    top = ms[0]
    for m in ms[1:]:
        top = jnp.maximum(top, m)
    scales = [jnp.exp2(m - top) for m in ms]
    total = scales[0] * ls[0]
    for a, l in zip(scales[1:], ls[1:]):
        total = total + a * l
    merged = None
    for a, o_ref, o_stage, dil in zip(scales, o_refs, o_stages, DILATIONS):
        w = a / total
        w_hi = w.astype(BF16)
        w_lo = (w - w_hi.astype(F32)).astype(BF16)
        w_full = _dot(jnp.concatenate([w_hi, w_lo], axis=-1), expand_ref[...])
        term = w_full * _load_natural(o_ref, o_stage, dil)
        merged = term if merged is None else merged + term
    y = _dot(merged.astype(BF16), wo_ref[...])
    out_ref[...] = x_ref[...] + _rms_norm(y, g1_ref[...])


def _merge_oproj(x, outs, ms, ls, wo, g1, tm):
    t, d = x.shape
    expand = np.zeros((2 * LANES, d), np.float32)
    for head in range(N_HEADS):
        expand[head, head * HEAD_DIM:(head + 1) * HEAD_DIM] = 1.0
        expand[LANES + head, head * HEAD_DIM:(head + 1) * HEAD_DIM] = 1.0
    view = lambda w: [pl.BlockSpec((tm // dil, dil * w), lambda i: (i, 0)) for dil in DILATIONS]
    return pl.pallas_call(
        _merge_oproj_kernel,
        grid=(t // tm,),
        in_specs=[pl.BlockSpec((tm, d), lambda i: (i, 0))] + view(d) + view(LANES) + view(LANES)
                 + [_resident((2 * LANES, d)), _resident((d, d)), _resident((1, d))],
        out_specs=pl.BlockSpec((tm, d), lambda i: (i, 0)),
        out_shape=jax.ShapeDtypeStruct((t, d), F32),
        scratch_shapes=[pltpu.VMEM((d // LANES, tm, LANES), F32)] * len(DILATIONS)
                       + [pltpu.VMEM((1, tm, LANES), F32)] * (2 * len(DILATIONS)),
        compiler_params=_params(1),
        name="merge_oproj",
    )(x, *outs, *ms, *ls, jnp.asarray(expand, BF16), wo, g1)


def kernel(x, norm_g, conv_in_w, conv_w, conv_out_w, kv_norm_g, kv_w, q_w, o_w, ffn_in_w, ffn_out_w):
    b, s, d = x.shape
    depth = norm_g.shape[0]
    n_conv = conv_in_w.shape[0]
    assert d == N_HEADS * HEAD_DIM and s % (max(DILATIONS) * ATTN_TILE) == 0
    tm = PROJ_TILE
    row = lambda g: g.reshape(1, d)
    k_views = v_views = None
    for layer in range(depth):
        g = norm_g[layer]
        if layer < n_conv:
            x = _conv_mixer(x, row(g[0]), row(g[1]), conv_in_w[layer].astype(BF16), conv_w[layer],
                            conv_out_w[layer].astype(BF16), DENSE_TILE)
            x2 = x.reshape(b * s, d)
        else:
            j = layer - n_conv
            x2 = x.reshape(b * s, d)
            if k_views is None:
                q_views, k_views, v_views = _qkv_proj(x2, row(g[0]), row(kv_norm_g),
                                                      q_w[j].astype(BF16), kv_w.astype(BF16), tm)
            else:
                q_views = _q_proj(x2, row(g[0]), q_w[j].astype(BF16), tm)
            branches = [_attention_branch(q, k, v, dil, b)
                        for q, k, v, dil in zip(q_views, k_views, v_views, DILATIONS)]
            x2 = _merge_oproj(x2, [o for o, _, _ in branches], [m for _, m, _ in branches],
                              [l for _, _, l in branches], o_w[j].astype(BF16), row(g[1]), tm)
        x2 = _ffn(x2, row(g[2]), row(g[3]), ffn_in_w[layer].astype(BF16),
                  ffn_out_w[layer].astype(BF16), DENSE_TILE)
        x = x2.reshape(b, s, d)
    return x
```

```python
import math

import numpy as np
import jax
import jax.numpy as jnp
from jax import lax
from jax.experimental import pallas as pl
from jax.experimental.pallas import tpu as pltpu

N_HEADS = 16
HEAD_DIM = 64
CONV_WIDTH = 3
DILATIONS = (1, 4, 16)
BLOCK = 128
EPS = 1e-6
MASKED = -1e30
LOG2E = math.log2(math.e)

LANES = 128
MXU_WIDTH = 256
ATTN_BLOCKS = 16
PROJ_TILE = 512
DENSE_TILE = 512

assert DILATIONS[0] == 1
CARRY_ROWS = 8
VMEM_LIMIT = 56 * 1024 * 1024
HEADS_PER_VREG = LANES // HEAD_DIM
N_PAIRS = N_HEADS // HEADS_PER_VREG

F32 = jnp.float32
BF16 = jnp.bfloat16


def _rms_norm(x, g):
    return x * lax.rsqrt(jnp.mean(x * x, axis=-1, keepdims=True) + EPS) * g


def _dot(a, b):
    return jnp.dot(a, b, preferred_element_type=F32)


def _resident(shape):
    return pl.BlockSpec(shape, lambda *_: (0,) * len(shape), pipeline_mode=pl.Buffered(1))


def _params(n_axes):
    return pltpu.CompilerParams(dimension_semantics=("arbitrary",) * n_axes,
                                vmem_limit_bytes=VMEM_LIMIT)


def _conv_mixer_kernel(x_ref, g0_ref, g1_ref, win_ref, cw_ref, wout_ref, o_ref, u_ref):
    tm, d = x_ref.shape[1], x_ref.shape[2]

    @pl.when(pl.program_id(1) == 0)
    def _():
        u_ref[0:CARRY_ROWS, :] = jnp.zeros((CARRY_ROWS, d), F32)

    @pl.when(pl.program_id(1) != 0)
    def _():
        u_ref[0:CARRY_ROWS, :] = u_ref[tm:tm + CARRY_ROWS, :]

    x = x_ref[0]
    xn = _rms_norm(x, g0_ref[...]).astype(BF16)
    c_gate = _dot(xn, win_ref[:, d:2 * d])
    h = _dot(xn, win_ref[:, 2 * d:3 * d])
    u_ref[CARRY_ROWS:CARRY_ROWS + tm, :] = c_gate * h
    conv = cw_ref[CONV_WIDTH - 1:CONV_WIDTH, :] * u_ref[CARRY_ROWS:CARRY_ROWS + tm, :]
    for k in range(CONV_WIDTH - 1):
        back = CONV_WIDTH - 1 - k
        conv = conv + cw_ref[k:k + 1, :] * u_ref[CARRY_ROWS - back:CARRY_ROWS - back + tm, :]
    b_gate = _dot(xn, win_ref[:, 0:d])
    y = _dot((b_gate * conv).astype(BF16), wout_ref[...])
    o_ref[0] = x + _rms_norm(y, g1_ref[...])


def _conv_mixer(x, g0, g1, w_in, w_conv, w_out, tm):
    b, s, d = x.shape
    return pl.pallas_call(
        _conv_mixer_kernel,
        grid=(b, s // tm),
        in_specs=[
            pl.BlockSpec((1, tm, d), lambda i, j: (i, j, 0)),
            _resident((1, d)), _resident((1, d)),
            _resident((d, 3 * d)), _resident((CONV_WIDTH, d)), _resident((d, d)),
        ],
        out_specs=pl.BlockSpec((1, tm, d), lambda i, j: (i, j, 0)),
        out_shape=jax.ShapeDtypeStruct((b, s, d), F32),
        scratch_shapes=[pltpu.VMEM((tm + CARRY_ROWS, d), F32)],
        compiler_params=_params(2),
        name="conv_mixer",
    )(x, g0, g1, w_in, w_conv, w_out)


def _ffn_kernel(x_ref, g2_ref, g3_ref, win_ref, wout_ref, o_ref):
    d_ff = wout_ref.shape[0]
    x = x_ref[...]
    hn = _rms_norm(x, g2_ref[...]).astype(BF16)
    gate = _dot(hn, win_ref[:, 0:d_ff])
    up = _dot(hn, win_ref[:, d_ff:2 * d_ff])
    act = (gate * jax.nn.sigmoid(gate) * up).astype(BF16)
    ff = _dot(act, wout_ref[...])
    o_ref[...] = x + _rms_norm(ff, g3_ref[...])


def _ffn(x, g2, g3, w_in, w_out, tm):
    t, d = x.shape
    d_ff = w_out.shape[0]
    return pl.pallas_call(
        _ffn_kernel,
        grid=(t // tm,),
        in_specs=[
            pl.BlockSpec((tm, d), lambda i: (i, 0)),
            _resident((1, d)), _resident((1, d)),
            _resident((d, 2 * d_ff)), _resident((d_ff, d)),
        ],
        out_specs=pl.BlockSpec((tm, d), lambda i: (i, 0)),
        out_shape=jax.ShapeDtypeStruct((t, d), F32),
        compiler_params=_params(1),
        name="ffn",
    )(x, g2, g3, w_in, w_out)


def _project_to_views(xn, w_ref, col0, scale, stage_ref, out_refs):
    n_chunks, tm, _ = stage_ref.shape
    d_model = n_chunks * LANES
    for n0 in range(0, d_model, MXU_WIDTH):
        val = _dot(xn, w_ref[:, col0 + n0:col0 + n0 + MXU_WIDTH])
        if scale is not None:
            val = val * scale
        out_refs[0][:, n0:n0 + MXU_WIDTH] = val.astype(BF16)
        chunks = range(n0 // LANES, (n0 + MXU_WIDTH) // LANES)
        for c in chunks:
            stage_ref[c] = val[:, c * LANES - n0:(c + 1) * LANES - n0]
        for dil, ref in zip(DILATIONS[1:], out_refs[1:]):
            for r in range(dil):
                for c in chunks:
                    rows = stage_ref[c, pl.ds(r, tm // dil, stride=dil), :]
                    col = r * d_model + c * LANES
                    ref[:, col:col + LANES] = rows.astype(BF16)


Q_SCALE = HEAD_DIM ** -0.5 * LOG2E


def _qkv_kernel(x_ref, g0_ref, gkv_ref, wq_ref, wkv_ref, *refs):
    n = len(DILATIONS)
    q_refs, k_refs, v_refs = refs[0:n], refs[n:2 * n], refs[2 * n:3 * n]
    q_stage, k_stage, v_stage = refs[3 * n:]
    d = x_ref.shape[1]
    x = x_ref[...]
    xr = x * lax.rsqrt(jnp.mean(x * x, axis=-1, keepdims=True) + EPS)
    _project_to_views((xr * g0_ref[...]).astype(BF16), wq_ref, 0, Q_SCALE, q_stage, q_refs)
    xkv = (xr * gkv_ref[...]).astype(BF16)
    _project_to_views(xkv, wkv_ref, 0, None, k_stage, k_refs)
    _project_to_views(xkv, wkv_ref, d, None, v_stage, v_refs)


def _q_kernel(x_ref, g0_ref, wq_ref, *refs):
    q_refs, stage_ref = refs[:-1], refs[-1]
    xn = _rms_norm(x_ref[...], g0_ref[...]).astype(BF16)
    _project_to_views(xn, wq_ref, 0, Q_SCALE, stage_ref, q_refs)


def _view_specs(t, d, tm):
    specs = [pl.BlockSpec((tm // dil, dil * d), lambda i: (i, 0)) for dil in DILATIONS]
    shapes = [jax.ShapeDtypeStruct((t // dil, dil * d), BF16) for dil in DILATIONS]
    return specs, shapes


def _qkv_proj(x, g0, gkv, wq, wkv, tm):
    t, d = x.shape
    specs, shapes = _view_specs(t, d, tm)
    n = len(DILATIONS)
    outs = pl.pallas_call(
        _qkv_kernel,
        grid=(t // tm,),
        in_specs=[pl.BlockSpec((tm, d), lambda i: (i, 0)), _resident((1, d)), _resident((1, d)),
                  _resident((d, d)), _resident((d, 2 * d))],
        out_specs=specs * 3,
        out_shape=shapes * 3,
        scratch_shapes=[pltpu.VMEM((d // LANES, tm, LANES), F32)] * 3,
        compiler_params=_params(1),
        name="qkv_proj",
    )(x, g0, gkv, wq, wkv)
    return outs[0:n], outs[n:2 * n], outs[2 * n:3 * n]


def _q_proj(x, g0, wq, tm):
    t, d = x.shape
    specs, shapes = _view_specs(t, d, tm)
    return pl.pallas_call(
        _q_kernel,
        grid=(t // tm,),
        in_specs=[pl.BlockSpec((tm, d), lambda i: (i, 0)), _resident((1, d)), _resident((d, d))],
        out_specs=specs,
        out_shape=shapes,
        scratch_shapes=[pltpu.VMEM((d // LANES, tm, LANES), F32)],
        compiler_params=_params(1),
        name="q_proj",
    )(x, g0, wq)


def _attn_kernel(q_ref, kp_ref, kc_ref, vp_ref, vc_ref, bias_ref, o_ref, m_ref, l_ref):
    d_model = N_HEADS * HEAD_DIM
    lane = lax.broadcasted_iota(jnp.int32, (BLOCK, LANES), 1)
    first_head_lanes = lane < HEAD_DIM
    sequence_start_table = jnp.minimum(pl.program_id(0), 1)
    for res in range(q_ref.shape[1] // d_model):
        for sb in range(q_ref.shape[0] // BLOCK):
            rows = slice(sb * BLOCK, (sb + 1) * BLOCK)
            m_tile = jnp.zeros((BLOCK, LANES), F32)
            l_tile = jnp.ones((BLOCK, LANES), F32)
            for pair in range(N_PAIRS):
                cols = slice(res * d_model + pair * LANES, res * d_model + (pair + 1) * LANES)
                q2 = q_ref[rows, cols]
                if sb == 0:
                    k2 = jnp.concatenate([kp_ref[:, cols], kc_ref[rows, cols]], axis=0)
                    v2 = jnp.concatenate([vp_ref[:, cols], vc_ref[rows, cols]], axis=0)
                    table = sequence_start_table
                else:
                    k2 = kc_ref[(sb - 1) * BLOCK:(sb + 1) * BLOCK, cols]
                    v2 = vc_ref[(sb - 1) * BLOCK:(sb + 1) * BLOCK, cols]
                    table = 1
                zero = jnp.zeros_like(q2)
                q_rows = jnp.concatenate([jnp.where(first_head_lanes, q2, zero),
                                          jnp.where(first_head_lanes, zero, q2)], axis=0)
                sc = lax.dot_general(q_rows, k2, (((1,), (1,)), ((), ())), preferred_element_type=F32)
                sc = sc + bias_ref[table, pair]
                m = jnp.max(sc, axis=-1, keepdims=True)
                p = jnp.exp2(sc - m)
                den = jnp.sum(p, axis=-1, keepdims=True)
                pv = _dot(p.astype(BF16), v2)
                o_ref[rows, cols] = jnp.where(first_head_lanes, pv[0:BLOCK],
                                              pv[BLOCK:2 * BLOCK]).astype(BF16)
                for half in range(HEADS_PER_VREG):
                    head = pair * HEADS_PER_VREG + half
                    half_rows = slice(half * BLOCK, (half + 1) * BLOCK)
                    m_tile = jnp.where(lane == head, m[half_rows], m_tile)
                    l_tile = jnp.where(lane == head, den[half_rows], l_tile)
            m_ref[rows, res * LANES:(res + 1) * LANES] = m_tile
            l_ref[rows, res * LANES:(res + 1) * LANES] = l_tile


def _attention_bias(dilation):
    heads = np.arange(N_HEADS, dtype=np.float64) + 1.0
    slopes = np.power(2.0, -8.0 * heads / N_HEADS)
    u = np.arange(BLOCK)[:, None]
    kk = np.arange(2 * BLOCK)[None, :]
    delta = u + BLOCK - kk
    band = (delta >= 0) & (delta <= BLOCK)
    bias = -slopes[:, None, None] * (delta * dilation)[None] * LOG2E
    later = np.where(band[None], bias, MASKED)
    first = np.where((band & (kk >= BLOCK))[None], bias, MASKED)
    stacked = np.stack([first, later]).reshape(2, N_PAIRS, HEADS_PER_VREG * BLOCK, 2 * BLOCK)
    return jnp.asarray(stacked.astype(np.float32))


def _attention_branch(q, k, v, dilation, n_seq):
    rows, width = q.shape
    d = width // dilation
    per_tile = min(ATTN_BLOCKS, rows // n_seq // BLOCK)
    n_res = min(ATTN_BLOCKS // per_tile, dilation)
    tq = per_tile * BLOCK
    nt = rows // n_seq // tq
    own = lambda w: pl.BlockSpec((tq, n_res * w), lambda n, i, r: (i * nt + n, r))
    prev = pl.BlockSpec((BLOCK, n_res * d),
                        lambda n, i, r: (jnp.maximum((i * nt + n) * per_tile - 1, 0), r))
    stat = jax.ShapeDtypeStruct((rows, dilation * LANES), F32)
    return pl.pallas_call(
        _attn_kernel,
        grid=(nt, n_seq, dilation // n_res),
        in_specs=[own(d), prev, own(d), prev, own(d),
                  _resident((2, N_PAIRS, HEADS_PER_VREG * BLOCK, 2 * BLOCK))],
        out_specs=[own(d), own(LANES), own(LANES)],
        out_shape=[jax.ShapeDtypeStruct((rows, width), BF16), stat, stat],
        compiler_params=_params(3),
        name=f"attention_d{dilation}",
    )(q, k, k, v, v, _attention_bias(dilation))


def _load_natural(view_ref, stage_ref, dil):
    if dil == 1:
        return view_ref[...].astype(F32)
    n_chunks, tm, _ = stage_ref.shape
    w = n_chunks * LANES
    for r in range(dil):
        for c in range(n_chunks):
            col = r * w + c * LANES
            stage_ref[c, pl.ds(r, tm // dil, stride=dil), :] = view_ref[:, col:col + LANES].astype(F32)
    if n_chunks == 1:
        return stage_ref[0]
    return jnp.concatenate([stage_ref[c] for c in range(n_chunks)], axis=-1)


def _merge_oproj_kernel(x_ref, *refs):
    n = len(DILATIONS)
    o_refs, m_refs, l_refs = refs[0:n], refs[n:2 * n], refs[2 * n:3 * n]
    expand_ref, wo_ref, g1_ref, out_ref = refs[3 * n:3 * n + 4]
    stages = refs[3 * n + 4:]
    o_stages, m_stages, l_stages = stages[0:n], stages[n:2 * n], stages[2 * n:3 * n]
    ms = [_load_natural(r, s, dil) for r, s, dil in zip(m_refs, m_stages, DILATIONS)]
    ls = [_load_natural(r, s, dil) for r, s, dil in zip(l_refs, l_stages, DILATIONS)]
    top = ms[0]
    for m in ms[1:]:
        top = jnp.maximum(top, m)
    scales = [jnp.exp2(m - top) for m in ms]
    total = scales[0] * ls[0]
    for a, l in zip(scales[1:], ls[1:]):
        total = total + a * l
    merged = None
    for a, o_ref, o_stage, dil in zip(scales, o_refs, o_stages, DILATIONS):
        w = a / total
        w_hi = w.astype(BF16)
        w_lo = (w - w_hi.astype(F32)).astype(BF16)
        w_full = _dot(jnp.concatenate([w_hi, w_lo], axis=-1), expand_ref[...])
        term = w_full * _load_natural(o_ref, o_stage, dil)
        merged = term if merged is None else merged + term
    y = _dot(merged.astype(BF16), wo_ref[...])
    out_ref[...] = x_ref[...] + _rms_norm(y, g1_ref[...])


def _merge_oproj(x, outs, ms, ls, wo, g1, tm):
    t, d = x.shape
    expand = np.zeros((2 * LANES, d), np.float32)
    for head in range(N_HEADS):
        expand[head, head * HEAD_DIM:(head + 1) * HEAD_DIM] = 1.0
        expand[LANES + head, head * HEAD_DIM:(head + 1) * HEAD_DIM] = 1.0
    view = lambda w: [pl.BlockSpec((tm // dil, dil * w), lambda i: (i, 0)) for dil in DILATIONS]
    return pl.pallas_call(
        _merge_oproj_kernel,
        grid=(t // tm,),
        in_specs=[pl.BlockSpec((tm, d), lambda i: (i, 0))] + view(d) + view(LANES) + view(LANES)
                 + [_resident((2 * LANES, d)), _resident((d, d)), _resident((1, d))],
        out_specs=pl.BlockSpec((tm, d), lambda i: (i, 0)),
        out_shape=jax.ShapeDtypeStruct((t, d), F32),
        scratch_shapes=[pltpu.VMEM((d // LANES, tm, LANES), F32)] * len(DILATIONS)
                       + [pltpu.VMEM((1, tm, LANES), F32)] * (2 * len(DILATIONS)),
        compiler_params=_params(1),
        name="merge_oproj",
    )(x, *outs, *ms, *ls, jnp.asarray(expand, BF16), wo, g1)


def kernel(x, norm_g, conv_in_w, conv_w, conv_out_w, kv_norm_g, kv_w, q_w, o_w, ffn_in_w, ffn_out_w):
    b, s, d = x.shape
    depth = norm_g.shape[0]
    n_conv = conv_in_w.shape[0]
    assert d == N_HEADS * HEAD_DIM and s % (max(DILATIONS) * BLOCK) == 0
    assert (b * s) % max(PROJ_TILE, DENSE_TILE) == 0 and s % DENSE_TILE == 0
    tm = PROJ_TILE
    row = lambda g: g.reshape(1, d)
    k_views = v_views = None
    for layer in range(depth):
        g = norm_g[layer]
        if layer < n_conv:
            x = _conv_mixer(x, row(g[0]), row(g[1]), conv_in_w[layer].astype(BF16), conv_w[layer],
                            conv_out_w[layer].astype(BF16), DENSE_TILE)
            x2 = x.reshape(b * s, d)
        else:
            j = layer - n_conv
            x2 = x.reshape(b * s, d)
            if k_views is None:
                q_views, k_views, v_views = _qkv_proj(x2, row(g[0]), row(kv_norm_g),
                                                      q_w[j].astype(BF16), kv_w.astype(BF16), tm)
            else:
                q_views = _q_proj(x2, row(g[0]), q_w[j].astype(BF16), tm)
            branches = [_attention_branch(q, k, v, dil, b)
                        for q, k, v, dil in zip(q_views, k_views, v_views, DILATIONS)]
            x2 = _merge_oproj(x2, [o for o, _, _ in branches], [m for _, m, _ in branches],
                              [l for _, _, l in branches], o_w[j].astype(BF16), row(g[1]), tm)
        x2 = _ffn(x2, row(g[2]), row(g[3]), ffn_in_w[layer].astype(BF16),
                  ffn_out_w[layer].astype(BF16), DENSE_TILE)
        x = x2.reshape(b, s, d)
    return x
```

```python
import math

import numpy as np
import jax
import jax.numpy as jnp
from jax import lax
from jax.experimental import pallas as pl
from jax.experimental.pallas import tpu as pltpu

N_HEADS = 16
HEAD_DIM = 64
CONV_WIDTH = 3
DILATIONS = (1, 4, 16)
BLOCK = 128
EPS = 1e-6
MASKED = -1e30
LOG2E = math.log2(math.e)

LANES = 128
MXU_WIDTH = 256
ATTN_BLOCKS = 16
PROJ_TILE = 512
DENSE_TILE = 1024
SUB_ROWS = 256

assert DILATIONS[0] == 1
CARRY_ROWS = 8
VMEM_LIMIT = 56 * 1024 * 1024
HEADS_PER_VREG = LANES // HEAD_DIM
N_PAIRS = N_HEADS // HEADS_PER_VREG

F32 = jnp.float32
BF16 = jnp.bfloat16


def _rms_norm(x, g):
    return x * lax.rsqrt(jnp.mean(x * x, axis=-1, keepdims=True) + EPS) * g


def _dot(a, b):
    return jnp.dot(a, b, preferred_element_type=F32)


def _resident(shape):
    return pl.BlockSpec(shape, lambda *_: (0,) * len(shape), pipeline_mode=pl.Buffered(1))


def _whole(array):
    return array, _resident(array.shape)


def _layer(stacked, index):
    shape = stacked.shape[1:]
    spec = pl.BlockSpec((None,) + shape, lambda *_: (index,) + (0,) * len(shape),
                        pipeline_mode=pl.Buffered(1))
    return stacked, spec


def _split(params):
    return [a for a, _ in params], [s for _, s in params]


def _params(n_axes):
    return pltpu.CompilerParams(dimension_semantics=("arbitrary",) * n_axes,
                                vmem_limit_bytes=VMEM_LIMIT)


def _conv_mixer_kernel(x_ref, g0_ref, g1_ref, win_ref, cw_ref, wout_ref, o_ref, u_ref):
    tm, d = x_ref.shape[1], x_ref.shape[2]

    @pl.when(pl.program_id(1) == 0)
    def _():
        u_ref[0:CARRY_ROWS, :] = jnp.zeros((CARRY_ROWS, d), F32)

    @pl.when(pl.program_id(1) != 0)
    def _():
        u_ref[0:CARRY_ROWS, :] = u_ref[tm:tm + CARRY_ROWS, :]

    for r0 in range(0, tm, SUB_ROWS):
        x = x_ref[0, r0:r0 + SUB_ROWS, :]
        xn = _rms_norm(x, g0_ref[...]).astype(BF16)
        c_gate = _dot(xn, win_ref[:, d:2 * d])
        h = _dot(xn, win_ref[:, 2 * d:3 * d])
        u0 = CARRY_ROWS + r0
        u_ref[u0:u0 + SUB_ROWS, :] = c_gate * h
        conv = cw_ref[CONV_WIDTH - 1:CONV_WIDTH, :] * u_ref[u0:u0 + SUB_ROWS, :]
        for k in range(CONV_WIDTH - 1):
            back = CONV_WIDTH - 1 - k
            conv = conv + cw_ref[k:k + 1, :] * u_ref[u0 - back:u0 - back + SUB_ROWS, :]
        b_gate = _dot(xn, win_ref[:, 0:d])
        y = _dot((b_gate * conv).astype(BF16), wout_ref[...])
        o_ref[0, r0:r0 + SUB_ROWS, :] = x + _rms_norm(y, g1_ref[...])


def _conv_mixer(x, params, tm):
    b, s, d = x.shape
    operands, specs = _split(params)
    return pl.pallas_call(
        _conv_mixer_kernel,
        grid=(b, s // tm),
        in_specs=[pl.BlockSpec((1, tm, d), lambda i, j: (i, j, 0))] + specs,
        out_specs=pl.BlockSpec((1, tm, d), lambda i, j: (i, j, 0)),
        out_shape=jax.ShapeDtypeStruct((b, s, d), F32),
        scratch_shapes=[pltpu.VMEM((tm + CARRY_ROWS, d), F32)],
        compiler_params=_params(2),
        name="conv_mixer",
    )(x, *operands)


def _ffn_kernel(x_ref, g2_ref, g3_ref, win_ref, wout_ref, o_ref):
    d_ff = wout_ref.shape[0]
    for r0 in range(0, x_ref.shape[0], SUB_ROWS):
        rows = slice(r0, r0 + SUB_ROWS)
        x = x_ref[rows, :]
        hn = _rms_norm(x, g2_ref[...]).astype(BF16)
        gate = _dot(hn, win_ref[:, 0:d_ff])
        up = _dot(hn, win_ref[:, d_ff:2 * d_ff])
        act = (gate * jax.nn.sigmoid(gate) * up).astype(BF16)
        ff = _dot(act, wout_ref[...])
        o_ref[rows, :] = x + _rms_norm(ff, g3_ref[...])


def _ffn(x, params, tm):
    t, d = x.shape
    operands, specs = _split(params)
    return pl.pallas_call(
        _ffn_kernel,
        grid=(t // tm,),
        in_specs=[pl.BlockSpec((tm, d), lambda i: (i, 0))] + specs,
        out_specs=pl.BlockSpec((tm, d), lambda i: (i, 0)),
        out_shape=jax.ShapeDtypeStruct((t, d), F32),
        compiler_params=_params(1),
        name="ffn",
    )(x, *operands)


def _project_to_views(xn, w_ref, col0, scale, stage_ref, out_refs):
    n_chunks, tm, _ = stage_ref.shape
    d_model = n_chunks * LANES
    for n0 in range(0, d_model, MXU_WIDTH):
        val = _dot(xn, w_ref[:, col0 + n0:col0 + n0 + MXU_WIDTH])
        if scale is not None:
            val = val * scale
        out_refs[0][:, n0:n0 + MXU_WIDTH] = val.astype(BF16)
        chunks = range(n0 // LANES, (n0 + MXU_WIDTH) // LANES)
        for c in chunks:
            stage_ref[c] = val[:, c * LANES - n0:(c + 1) * LANES - n0]
        for dil, ref in zip(DILATIONS[1:], out_refs[1:]):
            for r in range(dil):
                for c in chunks:
                    rows = stage_ref[c, pl.ds(r, tm // dil, stride=dil), :]
                    col = r * d_model + c * LANES
                    ref[:, col:col + LANES] = rows.astype(BF16)


Q_SCALE = HEAD_DIM ** -0.5 * LOG2E


def _qkv_kernel(x_ref, g0_ref, gkv_ref, wq_ref, wkv_ref, *refs):
    n = len(DILATIONS)
    q_refs, k_refs, v_refs = refs[0:n], refs[n:2 * n], refs[2 * n:3 * n]
    q_stage, k_stage, v_stage = refs[3 * n:]
    d = x_ref.shape[1]
    x = x_ref[...]
    xr = x * lax.rsqrt(jnp.mean(x * x, axis=-1, keepdims=True) + EPS)
    _project_to_views((xr * g0_ref[...]).astype(BF16), wq_ref, 0, Q_SCALE, q_stage, q_refs)
    xkv = (xr * gkv_ref[...]).astype(BF16)
    _project_to_views(xkv, wkv_ref, 0, None, k_stage, k_refs)
    _project_to_views(xkv, wkv_ref, d, None, v_stage, v_refs)


def _q_kernel(x_ref, g0_ref, wq_ref, *refs):
    q_refs, stage_ref = refs[:-1], refs[-1]
    xn = _rms_norm(x_ref[...], g0_ref[...]).astype(BF16)
    _project_to_views(xn, wq_ref, 0, Q_SCALE, stage_ref, q_refs)


def _view_specs(t, d, tm):
    specs = [pl.BlockSpec((tm // dil, dil * d), lambda i: (i, 0)) for dil in DILATIONS]
    shapes = [jax.ShapeDtypeStruct((t // dil, dil * d), BF16) for dil in DILATIONS]
    return specs, shapes


def _qkv_proj(x, params, tm):
    t, d = x.shape
    specs, shapes = _view_specs(t, d, tm)
    n = len(DILATIONS)
    operands, param_specs = _split(params)
    outs = pl.pallas_call(
        _qkv_kernel,
        grid=(t // tm,),
        in_specs=[pl.BlockSpec((tm, d), lambda i: (i, 0))] + param_specs,
        out_specs=specs * 3,
        out_shape=shapes * 3,
        scratch_shapes=[pltpu.VMEM((d // LANES, tm, LANES), F32)] * 3,
        compiler_params=_params(1),
        name="qkv_proj",
    )(x, *operands)
    return outs[0:n], outs[n:2 * n], outs[2 * n:3 * n]


def _q_proj(x, params, tm):
    t, d = x.shape
    specs, shapes = _view_specs(t, d, tm)
    operands, param_specs = _split(params)
    return pl.pallas_call(
        _q_kernel,
        grid=(t // tm,),
        in_specs=[pl.BlockSpec((tm, d), lambda i: (i, 0))] + param_specs,
        out_specs=specs,
        out_shape=shapes,
        scratch_shapes=[pltpu.VMEM((d // LANES, tm, LANES), F32)],
        compiler_params=_params(1),
        name="q_proj",
    )(x, *operands)


def _attn_kernel(q_ref, kp_ref, kc_ref, vp_ref, vc_ref, bias_ref, o_ref, m_ref, l_ref):
    d_model = N_HEADS * HEAD_DIM
    lane = lax.broadcasted_iota(jnp.int32, (BLOCK, LANES), 1)
    first_head_lanes = lane < HEAD_DIM
    sequence_start_table = jnp.minimum(pl.program_id(0), 1)
    for res in range(q_ref.shape[1] // d_model):
        for sb in range(q_ref.shape[0] // BLOCK):
            rows = slice(sb * BLOCK, (sb + 1) * BLOCK)
            m_tile = jnp.zeros((BLOCK, LANES), F32)
            l_tile = jnp.ones((BLOCK, LANES), F32)
            for pair in range(N_PAIRS):
                cols = slice(res * d_model + pair * LANES, res * d_model + (pair + 1) * LANES)
                q2 = q_ref[rows, cols]
                if sb == 0:
                    k2 = jnp.concatenate([kp_ref[:, cols], kc_ref[rows, cols]], axis=0)
                    v2 = jnp.concatenate([vp_ref[:, cols], vc_ref[rows, cols]], axis=0)
                    table = sequence_start_table
                else:
                    k2 = kc_ref[(sb - 1) * BLOCK:(sb + 1) * BLOCK, cols]
                    v2 = vc_ref[(sb - 1) * BLOCK:(sb + 1) * BLOCK, cols]
                    table = 1
                zero = jnp.zeros_like(q2)
                q_rows = jnp.concatenate([jnp.where(first_head_lanes, q2, zero),
                                          jnp.where(first_head_lanes, zero, q2)], axis=0)
                sc = lax.dot_general(q_rows, k2, (((1,), (1,)), ((), ())), preferred_element_type=F32)
                sc = sc + bias_ref[table, pair]
                m = jnp.max(sc, axis=-1, keepdims=True)
                p = jnp.exp2(sc - m)
                den = jnp.sum(p, axis=-1, keepdims=True)
                pv = _dot(p.astype(BF16), v2)
                o_ref[rows, cols] = jnp.where(first_head_lanes, pv[0:BLOCK],
                                              pv[BLOCK:2 * BLOCK]).astype(BF16)
                for half in range(HEADS_PER_VREG):
                    head = pair * HEADS_PER_VREG + half
                    half_rows = slice(half * BLOCK, (half + 1) * BLOCK)
                    m_tile = jnp.where(lane == head, m[half_rows], m_tile)
                    l_tile = jnp.where(lane == head, den[half_rows], l_tile)
            m_ref[rows, res * LANES:(res + 1) * LANES] = m_tile
            l_ref[rows, res * LANES:(res + 1) * LANES] = l_tile


def _attention_bias(dilation):
    heads = np.arange(N_HEADS, dtype=np.float64) + 1.0
    slopes = np.power(2.0, -8.0 * heads / N_HEADS)
    u = np.arange(BLOCK)[:, None]
    kk = np.arange(2 * BLOCK)[None, :]
    delta = u + BLOCK - kk
    band = (delta >= 0) & (delta <= BLOCK)
    bias = -slopes[:, None, None] * (delta * dilation)[None] * LOG2E
    later = np.where(band[None], bias, MASKED)
    first = np.where((band & (kk >= BLOCK))[None], bias, MASKED)
    stacked = np.stack([first, later]).reshape(2, N_PAIRS, HEADS_PER_VREG * BLOCK, 2 * BLOCK)
    return jnp.asarray(stacked.astype(np.float32))


def _attention_branch(q, k, v, dilation, n_seq):
    rows, width = q.shape
    d = width // dilation
    per_tile = min(ATTN_BLOCKS, rows // n_seq // BLOCK)
    n_res = min(ATTN_BLOCKS // per_tile, dilation)
    tq = per_tile * BLOCK
    nt = rows // n_seq // tq
    own = lambda w: pl.BlockSpec((tq, n_res * w), lambda n, i, r: (i * nt + n, r))
    prev = pl.BlockSpec((BLOCK, n_res * d),
                        lambda n, i, r: (jnp.maximum((i * nt + n) * per_tile - 1, 0), r))
    stat = jax.ShapeDtypeStruct((rows, dilation * LANES), F32)
    return pl.pallas_call(
        _attn_kernel,
        grid=(nt, n_seq, dilation // n_res),
        in_specs=[own(d), prev, own(d), prev, own(d),
                  _resident((2, N_PAIRS, HEADS_PER_VREG * BLOCK, 2 * BLOCK))],
        out_specs=[own(d), own(LANES), own(LANES)],
        out_shape=[jax.ShapeDtypeStruct((rows, width), BF16), stat, stat],
        compiler_params=_params(3),
        name=f"attention_d{dilation}",
    )(q, k, k, v, v, _attention_bias(dilation))


def _load_natural(view_ref, stage_ref, dil):
    if dil == 1:
        return view_ref[...].astype(F32)
    n_chunks, tm, _ = stage_ref.shape
    w = n_chunks * LANES
    for r in range(dil):
        for c in range(n_chunks):
            col = r * w + c * LANES
            stage_ref[c, pl.ds(r, tm // dil, stride=dil), :] = view_ref[:, col:col + LANES].astype(F32)
    if n_chunks == 1:
        return stage_ref[0]
    return jnp.concatenate([stage_ref[c] for c in range(n_chunks)], axis=-1)


def _merge_oproj_kernel(x_ref, *refs):
    n = len(DILATIONS)
    o_refs, m_refs, l_refs = refs[0:n], refs[n:2 * n], refs[2 * n:3 * n]
    expand_ref, wo_ref, g1_ref, out_ref = refs[3 * n:3 * n + 4]
    stages = refs[3 * n + 4:]
    o_stages, m_stages, l_stages = stages[0:n], stages[n:2 * n], stages[2 * n:3 * n]
    ms = [_load_natural(r, s, dil) for r, s, dil in zip(m_refs, m_stages, DILATIONS)]
    ls = [_load_natural(r, s, dil) for r, s, dil in zip(l_refs, l_stages, DILATIONS)]
    top = ms[0]
    for m in ms[1:]:
        top = jnp.maximum(top, m)
    scales = [jnp.exp2(m - top) for m in ms]
    total = scales[0] * ls[0]
    for a, l in zip(scales[1:], ls[1:]):
        total = total + a * l
    merged = None
    for a, o_ref, o_stage, dil in zip(scales, o_refs, o_stages, DILATIONS):
        w = a / total
        w_hi = w.astype(BF16)
        w_lo = (w - w_hi.astype(F32)).astype(BF16)
        w_full = _dot(jnp.concatenate([w_hi, w_lo], axis=-1), expand_ref[...])
        term = w_full * _load_natural(o_ref, o_stage, dil)
        merged = term if merged is None else merged + term
    y = _dot(merged.astype(BF16), wo_ref[...])
    out_ref[...] = x_ref[...] + _rms_norm(y, g1_ref[...])


def _merge_oproj(x, outs, ms, ls, params, tm):
    t, d = x.shape
    operands, param_specs = _split(params)
    expand = np.zeros((2 * LANES, d), np.float32)
    for head in range(N_HEADS):
        expand[head, head * HEAD_DIM:(head + 1) * HEAD_DIM] = 1.0
        expand[LANES + head, head * HEAD_DIM:(head + 1) * HEAD_DIM] = 1.0
    view = lambda w: [pl.BlockSpec((tm // dil, dil * w), lambda i: (i, 0)) for dil in DILATIONS]
    return pl.pallas_call(
        _merge_oproj_kernel,
        grid=(t // tm,),
        in_specs=[pl.BlockSpec((tm, d), lambda i: (i, 0))] + view(d) + view(LANES) + view(LANES)
                 + [_resident((2 * LANES, d))] + param_specs,
        out_specs=pl.BlockSpec((tm, d), lambda i: (i, 0)),
        out_shape=jax.ShapeDtypeStruct((t, d), F32),
        scratch_shapes=[pltpu.VMEM((d // LANES, tm, LANES), F32)] * len(DILATIONS)
                       + [pltpu.VMEM((1, tm, LANES), F32)] * (2 * len(DILATIONS)),
        compiler_params=_params(1),
        name="merge_oproj",
    )(x, *outs, *ms, *ls, jnp.asarray(expand, BF16), *operands)


def kernel(x, norm_g, conv_in_w, conv_w, conv_out_w, kv_norm_g, kv_w, q_w, o_w, ffn_in_w, ffn_out_w):
    b, s, d = x.shape
    depth = norm_g.shape[0]
    n_conv = conv_in_w.shape[0]
    assert d == N_HEADS * HEAD_DIM and s % (max(DILATIONS) * BLOCK) == 0
    assert s % DENSE_TILE == 0 and s % PROJ_TILE == 0 and DENSE_TILE % SUB_ROWS == 0
    tm = PROJ_TILE
    gains = norm_g.reshape(depth * norm_g.shape[1], 1, d)
    gain = lambda layer, i: _layer(gains, layer * norm_g.shape[1] + i)
    conv_in, conv_out = conv_in_w.astype(BF16), conv_out_w.astype(BF16)
    wq, wo = q_w.astype(BF16), o_w.astype(BF16)
    ffn_in, ffn_out = ffn_in_w.astype(BF16), ffn_out_w.astype(BF16)
    k_views = v_views = None
    for layer in range(depth):
        if layer < n_conv:
            x = _conv_mixer(x, [gain(layer, 0), gain(layer, 1), _layer(conv_in, layer),
                                _layer(conv_w, layer), _layer(conv_out, layer)], DENSE_TILE)
            x2 = x.reshape(b * s, d)
        else:
            j = layer - n_conv
            x2 = x.reshape(b * s, d)
            if k_views is None:
                q_views, k_views, v_views = _qkv_proj(
                    x2, [gain(layer, 0), _whole(kv_norm_g.reshape(1, d)), _layer(wq, j),
                         _whole(kv_w.astype(BF16))], tm)
            else:
                q_views = _q_proj(x2, [gain(layer, 0), _layer(wq, j)], tm)
            branches = [_attention_branch(q, k, v, dil, b)
                        for q, k, v, dil in zip(q_views, k_views, v_views, DILATIONS)]
            x2 = _merge_oproj(x2, [o for o, _, _ in branches], [m for _, m, _ in branches],
                              [l for _, _, l in branches], [_layer(wo, j), gain(layer, 1)], tm)
        x2 = _ffn(x2, [gain(layer, 2), gain(layer, 3), _layer(ffn_in, layer), _layer(ffn_out, layer)],
                  DENSE_TILE)
        x = x2.reshape(b, s, d)
    return x
```

```python
import math

import numpy as np
import jax
import jax.numpy as jnp
from jax import lax
from jax.experimental import pallas as pl
from jax.experimental.pallas import tpu as pltpu

N_HEADS = 16
HEAD_DIM = 64
CONV_WIDTH = 3
DILATIONS = (1, 4, 16)
BLOCK = 128
EPS = 1e-6
MASKED = -1e30
LOG2E = math.log2(math.e)

LANES = 128
MXU_WIDTH = 256
ATTN_BLOCKS = 16
PROJ_TILE = 512
DENSE_TILE = 1024
SUB_ROWS = 256
FREE_STRIDE = 4

assert DILATIONS[0] == 1
CARRY_ROWS = 8
VMEM_LIMIT = 56 * 1024 * 1024
HEADS_PER_VREG = LANES // HEAD_DIM
N_PAIRS = N_HEADS // HEADS_PER_VREG

F32 = jnp.float32
BF16 = jnp.bfloat16


def _rms_norm(x, g):
    return x * lax.rsqrt(jnp.mean(x * x, axis=-1, keepdims=True) + EPS) * g


def _dot(a, b):
    return jnp.dot(a, b, preferred_element_type=F32)


def _resident(shape):
    return pl.BlockSpec(shape, lambda *_: (0,) * len(shape), pipeline_mode=pl.Buffered(1))


def _whole(array):
    return array, _resident(array.shape)


def _layer(stacked, index):
    shape = stacked.shape[1:]
    spec = pl.BlockSpec((None,) + shape, lambda *_: (index,) + (0,) * len(shape),
                        pipeline_mode=pl.Buffered(1))
    return stacked, spec


def _split(params):
    return [a for a, _ in params], [s for _, s in params]


def _params(n_axes):
    return pltpu.CompilerParams(dimension_semantics=("arbitrary",) * n_axes,
                                vmem_limit_bytes=VMEM_LIMIT)


def _conv_mixer_kernel(x_ref, g0_ref, g1_ref, win_ref, cw_ref, wout_ref, o_ref, u_ref):
    tm, d = x_ref.shape[1], x_ref.shape[2]

    @pl.when(pl.program_id(1) == 0)
    def _():
        u_ref[0:CARRY_ROWS, :] = jnp.zeros((CARRY_ROWS, d), F32)

    @pl.when(pl.program_id(1) != 0)
    def _():
        u_ref[0:CARRY_ROWS, :] = u_ref[tm:tm + CARRY_ROWS, :]

    for r0 in range(0, tm, SUB_ROWS):
        x = x_ref[0, r0:r0 + SUB_ROWS, :]
        xn = _rms_norm(x, g0_ref[...]).astype(BF16)
        c_gate = _dot(xn, win_ref[:, d:2 * d])
        h = _dot(xn, win_ref[:, 2 * d:3 * d])
        u0 = CARRY_ROWS + r0
        u_ref[u0:u0 + SUB_ROWS, :] = c_gate * h
        conv = cw_ref[CONV_WIDTH - 1:CONV_WIDTH, :] * u_ref[u0:u0 + SUB_ROWS, :]
        for k in range(CONV_WIDTH - 1):
            back = CONV_WIDTH - 1 - k
            conv = conv + cw_ref[k:k + 1, :] * u_ref[u0 - back:u0 - back + SUB_ROWS, :]
        b_gate = _dot(xn, win_ref[:, 0:d])
        y = _dot((b_gate * conv).astype(BF16), wout_ref[...])
        o_ref[0, r0:r0 + SUB_ROWS, :] = x + _rms_norm(y, g1_ref[...])


def _conv_mixer(x, params, tm):
    b, s, d = x.shape
    operands, specs = _split(params)
    return pl.pallas_call(
        _conv_mixer_kernel,
        grid=(b, s // tm),
        in_specs=[pl.BlockSpec((1, tm, d), lambda i, j: (i, j, 0))] + specs,
        out_specs=pl.BlockSpec((1, tm, d), lambda i, j: (i, j, 0)),
        out_shape=jax.ShapeDtypeStruct((b, s, d), F32),
        scratch_shapes=[pltpu.VMEM((tm + CARRY_ROWS, d), F32)],
        compiler_params=_params(2),
        name="conv_mixer",
    )(x, *operands)


def _ffn_kernel(x_ref, g2_ref, g3_ref, win_ref, wout_ref, o_ref):
    d_ff = wout_ref.shape[0]
    for r0 in range(0, x_ref.shape[0], SUB_ROWS):
        rows = slice(r0, r0 + SUB_ROWS)
        x = x_ref[rows, :]
        hn = _rms_norm(x, g2_ref[...]).astype(BF16)
        gate = _dot(hn, win_ref[:, 0:d_ff])
        up = _dot(hn, win_ref[:, d_ff:2 * d_ff])
        act = (gate * jax.nn.sigmoid(gate) * up).astype(BF16)
        ff = _dot(act, wout_ref[...])
        o_ref[rows, :] = x + _rms_norm(ff, g3_ref[...])


def _ffn(x, params, tm):
    t, d = x.shape
    operands, specs = _split(params)
    return pl.pallas_call(
        _ffn_kernel,
        grid=(t // tm,),
        in_specs=[pl.BlockSpec((tm, d), lambda i: (i, 0))] + specs,
        out_specs=pl.BlockSpec((tm, d), lambda i: (i, 0)),
        out_shape=jax.ShapeDtypeStruct((t, d), F32),
        compiler_params=_params(1),
        name="ffn",
    )(x, *operands)


def _project_to_views(xn, w_ref, col0, scale, stage_ref, out_refs):
    n_chunks, tm, _ = stage_ref.shape
    d_model = n_chunks * LANES
    for n0 in range(0, d_model, MXU_WIDTH):
        val = _dot(xn, w_ref[:, col0 + n0:col0 + n0 + MXU_WIDTH])
        if scale is not None:
            val = val * scale
        out_refs[0][:, n0:n0 + MXU_WIDTH] = val.astype(BF16)
        chunks = range(n0 // LANES, (n0 + MXU_WIDTH) // LANES)
        for c in chunks:
            stage_ref[c] = val[:, c * LANES - n0:(c + 1) * LANES - n0]
        for dil, ref in zip(DILATIONS[1:], out_refs[1:]):
            for r in range(dil):
                for c in chunks:
                    rows = stage_ref[c, pl.ds(r, tm // dil, stride=dil), :]
                    col = r * d_model + c * LANES
                    ref[:, col:col + LANES] = rows.astype(BF16)


Q_SCALE = HEAD_DIM ** -0.5 * LOG2E


def _qkv_kernel(x_ref, g0_ref, gkv_ref, wq_ref, wkv_ref, *refs):
    n = len(DILATIONS)
    q_refs, k_refs, v_refs = refs[0:n], refs[n:2 * n], refs[2 * n:3 * n]
    q_stage, k_stage, v_stage = refs[3 * n:]
    d = x_ref.shape[1]
    x = x_ref[...]
    xr = x * lax.rsqrt(jnp.mean(x * x, axis=-1, keepdims=True) + EPS)
    _project_to_views((xr * g0_ref[...]).astype(BF16), wq_ref, 0, Q_SCALE, q_stage, q_refs)
    xkv = (xr * gkv_ref[...]).astype(BF16)
    _project_to_views(xkv, wkv_ref, 0, None, k_stage, k_refs)
    _project_to_views(xkv, wkv_ref, d, None, v_stage, v_refs)


def _q_kernel(x_ref, g0_ref, wq_ref, *refs):
    q_refs, stage_ref = refs[:-1], refs[-1]
    xn = _rms_norm(x_ref[...], g0_ref[...]).astype(BF16)
    _project_to_views(xn, wq_ref, 0, Q_SCALE, stage_ref, q_refs)


def _view_specs(t, d, tm):
    specs = [pl.BlockSpec((tm // dil, dil * d), lambda i: (i, 0)) for dil in DILATIONS]
    shapes = [jax.ShapeDtypeStruct((t // dil, dil * d), BF16) for dil in DILATIONS]
    return specs, shapes


def _qkv_proj(x, params, tm):
    t, d = x.shape
    specs, shapes = _view_specs(t, d, tm)
    n = len(DILATIONS)
    operands, param_specs = _split(params)
    outs = pl.pallas_call(
        _qkv_kernel,
        grid=(t // tm,),
        in_specs=[pl.BlockSpec((tm, d), lambda i: (i, 0))] + param_specs,
        out_specs=specs * 3,
        out_shape=shapes * 3,
        scratch_shapes=[pltpu.VMEM((d // LANES, tm, LANES), F32)] * 3,
        compiler_params=_params(1),
        name="qkv_proj",
    )(x, *operands)
    return outs[0:n], outs[n:2 * n], outs[2 * n:3 * n]


def _q_proj(x, params, tm):
    t, d = x.shape
    specs, shapes = _view_specs(t, d, tm)
    operands, param_specs = _split(params)
    return pl.pallas_call(
        _q_kernel,
        grid=(t // tm,),
        in_specs=[pl.BlockSpec((tm, d), lambda i: (i, 0))] + param_specs,
        out_specs=specs,
        out_shape=shapes,
        scratch_shapes=[pltpu.VMEM((d // LANES, tm, LANES), F32)],
        compiler_params=_params(1),
        name="q_proj",
    )(x, *operands)


def _attn_kernel(q_ref, kp_ref, kc_ref, vp_ref, vc_ref, bias_ref, o_ref, m_ref, l_ref):
    d_model = N_HEADS * HEAD_DIM
    lane = lax.broadcasted_iota(jnp.int32, (BLOCK, LANES), 1)
    first_head_lanes = lane < HEAD_DIM
    sequence_start_table = jnp.minimum(pl.program_id(0), 1)
    for res in range(q_ref.shape[1] // d_model):
        for sb in range(q_ref.shape[0] // BLOCK):
            rows = slice(sb * BLOCK, (sb + 1) * BLOCK)
            m_tile = jnp.zeros((BLOCK, LANES), F32)
            l_tile = jnp.ones((BLOCK, LANES), F32)
            for pair in range(N_PAIRS):
                cols = slice(res * d_model + pair * LANES, res * d_model + (pair + 1) * LANES)
                q2 = q_ref[rows, cols]
                if sb == 0:
                    k2 = jnp.concatenate([kp_ref[:, cols], kc_ref[rows, cols]], axis=0)
                    v2 = jnp.concatenate([vp_ref[:, cols], vc_ref[rows, cols]], axis=0)
                    table = sequence_start_table
                else:
                    k2 = kc_ref[(sb - 1) * BLOCK:(sb + 1) * BLOCK, cols]
                    v2 = vc_ref[(sb - 1) * BLOCK:(sb + 1) * BLOCK, cols]
                    table = 1
                zero = jnp.zeros_like(q2)
                q_rows = jnp.concatenate([jnp.where(first_head_lanes, q2, zero),
                                          jnp.where(first_head_lanes, zero, q2)], axis=0)
                sc = lax.dot_general(q_rows, k2, (((1,), (1,)), ((), ())), preferred_element_type=F32)
                sc = sc + bias_ref[table, pair]
                m = jnp.max(sc, axis=-1, keepdims=True)
                p = jnp.exp2(sc - m)
                den = jnp.sum(p, axis=-1, keepdims=True)
                pv = _dot(p.astype(BF16), v2)
                o_ref[rows, cols] = jnp.where(first_head_lanes, pv[0:BLOCK],
                                              pv[BLOCK:2 * BLOCK]).astype(BF16)
                for half in range(HEADS_PER_VREG):
                    head = pair * HEADS_PER_VREG + half
                    half_rows = slice(half * BLOCK, (half + 1) * BLOCK)
                    m_tile = jnp.where(lane == head, m[half_rows], m_tile)
                    l_tile = jnp.where(lane == head, den[half_rows], l_tile)
            m_ref[rows, res * LANES:(res + 1) * LANES] = m_tile
            l_ref[rows, res * LANES:(res + 1) * LANES] = l_tile


def _attention_bias(dilation):
    heads = np.arange(N_HEADS, dtype=np.float64) + 1.0
    slopes = np.power(2.0, -8.0 * heads / N_HEADS)
    u = np.arange(BLOCK)[:, None]
    kk = np.arange(2 * BLOCK)[None, :]
    delta = u + BLOCK - kk
    band = (delta >= 0) & (delta <= BLOCK)
    bias = -slopes[:, None, None] * (delta * dilation)[None] * LOG2E
    later = np.where(band[None], bias, MASKED)
    first = np.where((band & (kk >= BLOCK))[None], bias, MASKED)
    stacked = np.stack([first, later]).reshape(2, N_PAIRS, HEADS_PER_VREG * BLOCK, 2 * BLOCK)
    return jnp.asarray(stacked.astype(np.float32))


def _attention_branch(q, k, v, dilation, n_seq):
    rows, width = q.shape
    d = width // dilation
    per_tile = min(ATTN_BLOCKS, rows // n_seq // BLOCK)
    n_res = min(ATTN_BLOCKS // per_tile, dilation)
    tq = per_tile * BLOCK
    nt = rows // n_seq // tq
    own = lambda w: pl.BlockSpec((tq, n_res * w), lambda n, i, r: (i * nt + n, r))
    prev = pl.BlockSpec((BLOCK, n_res * d),
                        lambda n, i, r: (jnp.maximum((i * nt + n) * per_tile - 1, 0), r))
    stat = jax.ShapeDtypeStruct((rows, dilation * LANES), F32)
    return pl.pallas_call(
        _attn_kernel,
        grid=(nt, n_seq, dilation // n_res),
        in_specs=[own(d), prev, own(d), prev, own(d),
                  _resident((2, N_PAIRS, HEADS_PER_VREG * BLOCK, 2 * BLOCK))],
        out_specs=[own(d), own(LANES), own(LANES)],
        out_shape=[jax.ShapeDtypeStruct((rows, width), BF16), stat, stat],
        compiler_params=_params(3),
        name=f"attention_d{dilation}",
    )(q, k, k, v, v, _attention_bias(dilation))


def _load_natural(view_ref, stage_ref, dil):
    if dil == 1:
        return view_ref[...].astype(F32)
    _, n_chunks, tm, _ = stage_ref.shape
    w = n_chunks * LANES
    inner = min(dil, FREE_STRIDE)
    outer = dil // inner
    assert inner * outer == dil and outer <= FREE_STRIDE
    group = tm // inner
    for c in range(n_chunks):
        if outer > 1:
            for r in range(dil):
                col = r * w + c * LANES
                start = (r % inner) * group + r // inner
                stage_ref[1, c, pl.ds(start, tm // dil, stride=outer), :] = (
                    view_ref[:, col:col + LANES].astype(F32))
        for r_in in range(inner):
            if outer > 1:
                rows = stage_ref[1, c, r_in * group:(r_in + 1) * group, :]
            else:
                rows = view_ref[:, r_in * w + c * LANES:r_in * w + (c + 1) * LANES].astype(F32)
            stage_ref[0, c, pl.ds(r_in, group, stride=inner), :] = rows
    if n_chunks == 1:
        return stage_ref[0, 0]
    return jnp.concatenate([stage_ref[0, c] for c in range(n_chunks)], axis=-1)


def _merge_oproj_kernel(x_ref, *refs):
    n = len(DILATIONS)
    o_refs, m_refs, l_refs = refs[0:n], refs[n:2 * n], refs[2 * n:3 * n]
    expand_ref, wo_ref, g1_ref, out_ref = refs[3 * n:3 * n + 4]
    stages = refs[3 * n + 4:]
    o_stages, m_stages, l_stages = stages[0:n], stages[n:2 * n], stages[2 * n:3 * n]
    ms = [_load_natural(r, s, dil) for r, s, dil in zip(m_refs, m_stages, DILATIONS)]
    ls = [_load_natural(r, s, dil) for r, s, dil in zip(l_refs, l_stages, DILATIONS)]
    top = ms[0]
    for m in ms[1:]:
        top = jnp.maximum(top, m)
    scales = [jnp.exp2(m - top) for m in ms]
    total = scales[0] * ls[0]
    for a, l in zip(scales[1:], ls[1:]):
        total = total + a * l
    merged = None
    for a, o_ref, o_stage, dil in zip(scales, o_refs, o_stages, DILATIONS):
        w = a / total
        w_hi = w.astype(BF16)
        w_lo = (w - w_hi.astype(F32)).astype(BF16)
        w_full = _dot(jnp.concatenate([w_hi, w_lo], axis=-1), expand_ref[...])
        term = w_full * _load_natural(o_ref, o_stage, dil)
        merged = term if merged is None else merged + term
    y = _dot(merged.astype(BF16), wo_ref[...])
    out_ref[...] = x_ref[...] + _rms_norm(y, g1_ref[...])


def _merge_oproj(x, outs, ms, ls, params, tm):
    t, d = x.shape
    operands, param_specs = _split(params)
    expand = np.zeros((2 * LANES, d), np.float32)
    for head in range(N_HEADS):
        expand[head, head * HEAD_DIM:(head + 1) * HEAD_DIM] = 1.0
        expand[LANES + head, head * HEAD_DIM:(head + 1) * HEAD_DIM] = 1.0
    view = lambda w: [pl.BlockSpec((tm // dil, dil * w), lambda i: (i, 0)) for dil in DILATIONS]
    return pl.pallas_call(
        _merge_oproj_kernel,
        grid=(t // tm,),
        in_specs=[pl.BlockSpec((tm, d), lambda i: (i, 0))] + view(d) + view(LANES) + view(LANES)
                 + [_resident((2 * LANES, d))] + param_specs,
        out_specs=pl.BlockSpec((tm, d), lambda i: (i, 0)),
        out_shape=jax.ShapeDtypeStruct((t, d), F32),
        scratch_shapes=[pltpu.VMEM((2, d // LANES, tm, LANES), F32)] * len(DILATIONS)
                       + [pltpu.VMEM((2, 1, tm, LANES), F32)] * (2 * len(DILATIONS)),
        compiler_params=_params(1),
        name="merge_oproj",
    )(x, *outs, *ms, *ls, jnp.asarray(expand, BF16), *operands)


def kernel(x, norm_g, conv_in_w, conv_w, conv_out_w, kv_norm_g, kv_w, q_w, o_w, ffn_in_w, ffn_out_w):
    b, s, d = x.shape
    depth = norm_g.shape[0]
    n_conv = conv_in_w.shape[0]
    assert d == N_HEADS * HEAD_DIM and s % (max(DILATIONS) * BLOCK) == 0
    assert s % DENSE_TILE == 0 and s % PROJ_TILE == 0 and DENSE_TILE % SUB_ROWS == 0
    tm = PROJ_TILE
    gains = norm_g.reshape(depth * norm_g.shape[1], 1, d)
    gain = lambda layer, i: _layer(gains, layer * norm_g.shape[1] + i)
    conv_in, conv_out = conv_in_w.astype(BF16), conv_out_w.astype(BF16)
    wq, wo = q_w.astype(BF16), o_w.astype(BF16)
    ffn_in, ffn_out = ffn_in_w.astype(BF16), ffn_out_w.astype(BF16)
    k_views = v_views = None
    for layer in range(depth):
        if layer < n_conv:
            x = _conv_mixer(x, [gain(layer, 0), gain(layer, 1), _layer(conv_in, layer),
                                _layer(conv_w, layer), _layer(conv_out, layer)], DENSE_TILE)
            x2 = x.reshape(b * s, d)
        else:
            j = layer - n_conv
            x2 = x.reshape(b * s, d)
            if k_views is None:
                q_views, k_views, v_views = _qkv_proj(
                    x2, [gain(layer, 0), _whole(kv_norm_g.reshape(1, d)), _layer(wq, j),
                         _whole(kv_w.astype(BF16))], tm)
            else:
                q_views = _q_proj(x2, [gain(layer, 0), _layer(wq, j)], tm)
            branches = [_attention_branch(q, k, v, dil, b)
                        for q, k, v, dil in zip(q_views, k_views, v_views, DILATIONS)]
            x2 = _merge_oproj(x2, [o for o, _, _ in branches], [m for _, m, _ in branches],
                              [l for _, _, l in branches], [_layer(wo, j), gain(layer, 1)], tm)
        x2 = _ffn(x2, [gain(layer, 2), gain(layer, 3), _layer(ffn_in, layer), _layer(ffn_out, layer)],
                  DENSE_TILE)
        x = x2.reshape(b, s, d)
    return x
```

```python
import math

import numpy as np
import jax
import jax.numpy as jnp
from jax import lax
from jax.experimental import pallas as pl
from jax.experimental.pallas import tpu as pltpu

N_HEADS = 16
HEAD_DIM = 64
CONV_WIDTH = 3
DILATIONS = (1, 4, 16)
BLOCK = 128
EPS = 1e-6
MASKED = -1e30
LOG2E = math.log2(math.e)

LANES = 128
MXU_WIDTH = 256
ATTN_BLOCKS = 16
PROJ_TILE = 512
DENSE_TILE = 1024
SUB_ROWS = 256
FREE_STRIDE = 4

assert DILATIONS[0] == 1
CARRY_ROWS = 8
VMEM_LIMIT = 56 * 1024 * 1024
HEADS_PER_VREG = LANES // HEAD_DIM
N_PAIRS = N_HEADS // HEADS_PER_VREG

F32 = jnp.float32
BF16 = jnp.bfloat16


def _rms_norm(x, g):
    return x * lax.rsqrt(jnp.mean(x * x, axis=-1, keepdims=True) + EPS) * g


def _dot(a, b):
    return jnp.dot(a, b, preferred_element_type=F32)


def _resident(shape):
    return pl.BlockSpec(shape, lambda *_: (0,) * len(shape), pipeline_mode=pl.Buffered(1))


def _whole(array):
    return array, _resident(array.shape)


def _layer(stacked, index):
    shape = stacked.shape[1:]
    spec = pl.BlockSpec((None,) + shape, lambda *_: (index,) + (0,) * len(shape),
                        pipeline_mode=pl.Buffered(1))
    return stacked, spec


def _split(params):
    return [a for a, _ in params], [s for _, s in params]


def _params(n_axes):
    return pltpu.CompilerParams(dimension_semantics=("arbitrary",) * n_axes,
                                vmem_limit_bytes=VMEM_LIMIT)


def _conv_mixer_kernel(x_ref, g0_ref, g1_ref, win_ref, cw_ref, wout_ref, o_ref, u_ref):
    tm, d = x_ref.shape[1], x_ref.shape[2]

    @pl.when(pl.program_id(1) == 0)
    def _():
        u_ref[0:CARRY_ROWS, :] = jnp.zeros((CARRY_ROWS, d), F32)

    @pl.when(pl.program_id(1) != 0)
    def _():
        u_ref[0:CARRY_ROWS, :] = u_ref[tm:tm + CARRY_ROWS, :]

    for r0 in range(0, tm, SUB_ROWS):
        x = x_ref[0, r0:r0 + SUB_ROWS, :]
        xn = _rms_norm(x, g0_ref[...]).astype(BF16)
        c_gate = _dot(xn, win_ref[:, d:2 * d])
        h = _dot(xn, win_ref[:, 2 * d:3 * d])
        u0 = CARRY_ROWS + r0
        u_ref[u0:u0 + SUB_ROWS, :] = c_gate * h
        conv = cw_ref[CONV_WIDTH - 1:CONV_WIDTH, :] * u_ref[u0:u0 + SUB_ROWS, :]
        for k in range(CONV_WIDTH - 1):
            back = CONV_WIDTH - 1 - k
            conv = conv + cw_ref[k:k + 1, :] * u_ref[u0 - back:u0 - back + SUB_ROWS, :]
        b_gate = _dot(xn, win_ref[:, 0:d])
        y = _dot((b_gate * conv).astype(BF16), wout_ref[...])
        o_ref[0, r0:r0 + SUB_ROWS, :] = x + _rms_norm(y, g1_ref[...])


def _conv_mixer(x, params, tm):
    b, s, d = x.shape
    operands, specs = _split(params)
    return pl.pallas_call(
        _conv_mixer_kernel,
        grid=(b, s // tm),
        in_specs=[pl.BlockSpec((1, tm, d), lambda i, j: (i, j, 0))] + specs,
        out_specs=pl.BlockSpec((1, tm, d), lambda i, j: (i, j, 0)),
        out_shape=jax.ShapeDtypeStruct((b, s, d), F32),
        scratch_shapes=[pltpu.VMEM((tm + CARRY_ROWS, d), F32)],
        compiler_params=_params(2),
        name="conv_mixer",
    )(x, *operands)


def _ffn_kernel(x_ref, g2_ref, g3_ref, win_ref, wout_ref, o_ref):
    d_ff = wout_ref.shape[0]
    for r0 in range(0, x_ref.shape[0], SUB_ROWS):
        rows = slice(r0, r0 + SUB_ROWS)
        x = x_ref[rows, :]
        hn = _rms_norm(x, g2_ref[...]).astype(BF16)
        gate = _dot(hn, win_ref[:, 0:d_ff])
        up = _dot(hn, win_ref[:, d_ff:2 * d_ff])
        act = (gate * jax.nn.sigmoid(gate) * up).astype(BF16)
        ff = _dot(act, wout_ref[...])
        o_ref[rows, :] = x + _rms_norm(ff, g3_ref[...])


def _ffn(x, params, tm):
    t, d = x.shape
    operands, specs = _split(params)
    return pl.pallas_call(
        _ffn_kernel,
        grid=(t // tm,),
        in_specs=[pl.BlockSpec((tm, d), lambda i: (i, 0))] + specs,
        out_specs=pl.BlockSpec((tm, d), lambda i: (i, 0)),
        out_shape=jax.ShapeDtypeStruct((t, d), F32),
        compiler_params=_params(1),
        name="ffn",
    )(x, *operands)


def _project_to_views(xn, w_ref, col0, scale, stage_ref, out_refs):
    n_chunks, tm, _ = stage_ref.shape
    d_model = n_chunks * LANES
    for n0 in range(0, d_model, MXU_WIDTH):
        val = _dot(xn, w_ref[:, col0 + n0:col0 + n0 + MXU_WIDTH])
        if scale is not None:
            val = val * scale
        out_refs[0][:, n0:n0 + MXU_WIDTH] = val.astype(BF16)
        chunks = range(n0 // LANES, (n0 + MXU_WIDTH) // LANES)
        for c in chunks:
            stage_ref[c] = val[:, c * LANES - n0:(c + 1) * LANES - n0]
        for dil, ref in zip(DILATIONS[1:], out_refs[1:]):
            for r in range(dil):
                for c in chunks:
                    rows = stage_ref[c, pl.ds(r, tm // dil, stride=dil), :]
                    col = r * d_model + c * LANES
                    ref[:, col:col + LANES] = rows.astype(BF16)


Q_SCALE = HEAD_DIM ** -0.5 * LOG2E


def _qkv_kernel(x_ref, g0_ref, gkv_ref, wq_ref, wkv_ref, *refs):
    n = len(DILATIONS)
    q_refs, k_refs, v_refs = refs[0:n], refs[n:2 * n], refs[2 * n:3 * n]
    q_stage, k_stage, v_stage = refs[3 * n:]
    d = x_ref.shape[1]
    x = x_ref[...]
    xr = x * lax.rsqrt(jnp.mean(x * x, axis=-1, keepdims=True) + EPS)
    _project_to_views((xr * g0_ref[...]).astype(BF16), wq_ref, 0, Q_SCALE, q_stage, q_refs)
    xkv = (xr * gkv_ref[...]).astype(BF16)
    _project_to_views(xkv, wkv_ref, 0, None, k_stage, k_refs)
    _project_to_views(xkv, wkv_ref, d, None, v_stage, v_refs)


def _q_kernel(x_ref, g0_ref, wq_ref, *refs):
    q_refs, stage_ref = refs[:-1], refs[-1]
    xn = _rms_norm(x_ref[...], g0_ref[...]).astype(BF16)
    _project_to_views(xn, wq_ref, 0, Q_SCALE, stage_ref, q_refs)


def _view_specs(t, d, tm):
    specs = [pl.BlockSpec((tm // dil, dil * d), lambda i: (i, 0)) for dil in DILATIONS]
    shapes = [jax.ShapeDtypeStruct((t // dil, dil * d), BF16) for dil in DILATIONS]
    return specs, shapes


def _qkv_proj(x, params, tm):
    t, d = x.shape
    specs, shapes = _view_specs(t, d, tm)
    n = len(DILATIONS)
    operands, param_specs = _split(params)
    outs = pl.pallas_call(
        _qkv_kernel,
        grid=(t // tm,),
        in_specs=[pl.BlockSpec((tm, d), lambda i: (i, 0))] + param_specs,
        out_specs=specs * 3,
        out_shape=shapes * 3,
        scratch_shapes=[pltpu.VMEM((d // LANES, tm, LANES), F32)] * 3,
        compiler_params=_params(1),
        name="qkv_proj",
    )(x, *operands)
    return outs[0:n], outs[n:2 * n], outs[2 * n:3 * n]


def _q_proj(x, params, tm):
    t, d = x.shape
    specs, shapes = _view_specs(t, d, tm)
    operands, param_specs = _split(params)
    return pl.pallas_call(
        _q_kernel,
        grid=(t // tm,),
        in_specs=[pl.BlockSpec((tm, d), lambda i: (i, 0))] + param_specs,
        out_specs=specs,
        out_shape=shapes,
        scratch_shapes=[pltpu.VMEM((d // LANES, tm, LANES), F32)],
        compiler_params=_params(1),
        name="q_proj",
    )(x, *operands)


def _attn_kernel(q_ref, kp_ref, kc_ref, vp_ref, vc_ref, bias_ref, o_ref, s_ref):
    d_model = N_HEADS * HEAD_DIM
    lane = lax.broadcasted_iota(jnp.int32, (BLOCK, LANES), 1)
    first_head_lanes = lane < HEAD_DIM
    sequence_start_table = jnp.minimum(pl.program_id(0), 1)
    for res in range(q_ref.shape[1] // d_model):
        for sb in range(q_ref.shape[0] // BLOCK):
            rows = slice(sb * BLOCK, (sb + 1) * BLOCK)
            m_tile = jnp.zeros((BLOCK, LANES), F32)
            l_tile = jnp.zeros((BLOCK, LANES), F32)
            for pair in range(N_PAIRS):
                cols = slice(res * d_model + pair * LANES, res * d_model + (pair + 1) * LANES)
                q2 = q_ref[rows, cols]
                if sb == 0:
                    k2 = jnp.concatenate([kp_ref[:, cols], kc_ref[rows, cols]], axis=0)
                    v2 = jnp.concatenate([vp_ref[:, cols], vc_ref[rows, cols]], axis=0)
                    table = sequence_start_table
                else:
                    k2 = kc_ref[(sb - 1) * BLOCK:(sb + 1) * BLOCK, cols]
                    v2 = vc_ref[(sb - 1) * BLOCK:(sb + 1) * BLOCK, cols]
                    table = 1
                zero = jnp.zeros_like(q2)
                q_rows = jnp.concatenate([jnp.where(first_head_lanes, q2, zero),
                                          jnp.where(first_head_lanes, zero, q2)], axis=0)
                sc = lax.dot_general(q_rows, k2, (((1,), (1,)), ((), ())), preferred_element_type=F32)
                sc = sc + bias_ref[table, pair]
                m = jnp.max(sc, axis=-1, keepdims=True)
                p = jnp.exp2(sc - m)
                den = jnp.sum(p, axis=-1, keepdims=True)
                pv = _dot(p.astype(BF16), v2)
                o_ref[rows, cols] = jnp.where(first_head_lanes, pv[0:BLOCK],
                                              pv[BLOCK:2 * BLOCK]).astype(BF16)
                for half in range(HEADS_PER_VREG):
                    head = pair * HEADS_PER_VREG + half
                    half_rows = slice(half * BLOCK, (half + 1) * BLOCK)
                    m_tile = jnp.where(lane == head, m[half_rows], m_tile)
                    l_tile = jnp.where(lane == N_HEADS + head, den[half_rows], l_tile)
            s_ref[rows, res * LANES:(res + 1) * LANES] = jnp.where(lane < N_HEADS, m_tile, l_tile)


def _attention_bias(dilation):
    heads = np.arange(N_HEADS, dtype=np.float64) + 1.0
    slopes = np.power(2.0, -8.0 * heads / N_HEADS)
    u = np.arange(BLOCK)[:, None]
    kk = np.arange(2 * BLOCK)[None, :]
    delta = u + BLOCK - kk
    band = (delta >= 0) & (delta <= BLOCK)
    bias = -slopes[:, None, None] * (delta * dilation)[None] * LOG2E
    later = np.where(band[None], bias, MASKED)
    first = np.where((band & (kk >= BLOCK))[None], bias, MASKED)
    stacked = np.stack([first, later]).reshape(2, N_PAIRS, HEADS_PER_VREG * BLOCK, 2 * BLOCK)
    return jnp.asarray(stacked.astype(np.float32))


def _attention_branch(q, k, v, dilation, n_seq):
    rows, width = q.shape
    d = width // dilation
    per_tile = min(ATTN_BLOCKS, rows // n_seq // BLOCK)
    n_res = min(ATTN_BLOCKS // per_tile, dilation)
    tq = per_tile * BLOCK
    nt = rows // n_seq // tq
    own = lambda w: pl.BlockSpec((tq, n_res * w), lambda n, i, r: (i * nt + n, r))
    prev = pl.BlockSpec((BLOCK, n_res * d),
                        lambda n, i, r: (jnp.maximum((i * nt + n) * per_tile - 1, 0), r))
    stat = jax.ShapeDtypeStruct((rows, dilation * LANES), F32)
    return pl.pallas_call(
        _attn_kernel,
        grid=(nt, n_seq, dilation // n_res),
        in_specs=[own(d), prev, own(d), prev, own(d),
                  _resident((2, N_PAIRS, HEADS_PER_VREG * BLOCK, 2 * BLOCK))],
        out_specs=[own(d), own(LANES)],
        out_shape=[jax.ShapeDtypeStruct((rows, width), BF16), stat],
        compiler_params=_params(3),
        name=f"attention_d{dilation}",
    )(q, k, k, v, v, _attention_bias(dilation))


def _load_natural(view_ref, stage_ref, dil):
    if dil == 1:
        return view_ref[...].astype(F32)
    _, n_chunks, tm, _ = stage_ref.shape
    w = n_chunks * LANES
    inner = min(dil, FREE_STRIDE)
    outer = dil // inner
    assert inner * outer == dil and outer <= FREE_STRIDE
    group = tm // inner
    for c in range(n_chunks):
        if outer > 1:
            for r in range(dil):
                col = r * w + c * LANES
                start = (r % inner) * group + r // inner
                stage_ref[1, c, pl.ds(start, tm // dil, stride=outer), :] = (
                    view_ref[:, col:col + LANES].astype(F32))
        for r_in in range(inner):
            if outer > 1:
                rows = stage_ref[1, c, r_in * group:(r_in + 1) * group, :]
            else:
                rows = view_ref[:, r_in * w + c * LANES:r_in * w + (c + 1) * LANES].astype(F32)
            stage_ref[0, c, pl.ds(r_in, group, stride=inner), :] = rows
    if n_chunks == 1:
        return stage_ref[0, 0]
    return jnp.concatenate([stage_ref[0, c] for c in range(n_chunks)], axis=-1)


def _merge_oproj_kernel(x_ref, *refs):
    n = len(DILATIONS)
    o_refs, s_refs = refs[0:n], refs[n:2 * n]
    expand_ref, wo_ref, g1_ref, out_ref = refs[2 * n:2 * n + 4]
    stages = refs[2 * n + 4:]
    o_stages, s_stages = (None,) + stages[0:n - 1], (None,) + stages[n - 1:2 * n - 2]
    ms = [_load_natural(r, s, dil) for r, s, dil in zip(s_refs, s_stages, DILATIONS)]
    ls = [pltpu.roll(m, LANES - N_HEADS, axis=1) for m in ms]
    head_lanes = lax.broadcasted_iota(jnp.int32, ms[0].shape, 1) < N_HEADS
    top = ms[0]
    for m in ms[1:]:
        top = jnp.maximum(top, m)
    scales = [jnp.exp2(m - top) for m in ms]
    total = scales[0] * ls[0]
    for a, l in zip(scales[1:], ls[1:]):
        total = total + a * l
    merged = None
    for a, o_ref, o_stage, dil in zip(scales, o_refs, o_stages, DILATIONS):
        w = jnp.where(head_lanes, a / total, 0.0)
        w_hi = w.astype(BF16)
        w_lo = (w - w_hi.astype(F32)).astype(BF16)
        w_full = _dot(jnp.concatenate([w_hi, w_lo], axis=-1), expand_ref[...])
        term = w_full * _load_natural(o_ref, o_stage, dil)
        merged = term if merged is None else merged + term
    y = _dot(merged.astype(BF16), wo_ref[...])
    out_ref[...] = x_ref[...] + _rms_norm(y, g1_ref[...])


def _merge_oproj(x, outs, stats, params, tm):
    t, d = x.shape
    operands, param_specs = _split(params)
    expand = np.zeros((2 * LANES, d), np.float32)
    for head in range(N_HEADS):
        expand[head, head * HEAD_DIM:(head + 1) * HEAD_DIM] = 1.0
        expand[LANES + head, head * HEAD_DIM:(head + 1) * HEAD_DIM] = 1.0
    view = lambda w: [pl.BlockSpec((tm // dil, dil * w), lambda i: (i, 0)) for dil in DILATIONS]
    return pl.pallas_call(
        _merge_oproj_kernel,
        grid=(t // tm,),
        in_specs=[pl.BlockSpec((tm, d), lambda i: (i, 0))] + view(d) + view(LANES)
                 + [_resident((2 * LANES, d))] + param_specs,
        out_specs=pl.BlockSpec((tm, d), lambda i: (i, 0)),
        out_shape=jax.ShapeDtypeStruct((t, d), F32),
        scratch_shapes=[pltpu.VMEM((2, d // LANES, tm, LANES), F32)] * (len(DILATIONS) - 1)
                       + [pltpu.VMEM((2, 1, tm, LANES), F32)] * (len(DILATIONS) - 1),
        compiler_params=_params(1),
        name="merge_oproj",
    )(x, *outs, *stats, jnp.asarray(expand, BF16), *operands)


def kernel(x, norm_g, conv_in_w, conv_w, conv_out_w, kv_norm_g, kv_w, q_w, o_w, ffn_in_w, ffn_out_w):
    b, s, d = x.shape
    depth = norm_g.shape[0]
    n_conv = conv_in_w.shape[0]
    assert d == N_HEADS * HEAD_DIM and s % (max(DILATIONS) * BLOCK) == 0
    assert s % DENSE_TILE == 0 and s % PROJ_TILE == 0 and DENSE_TILE % SUB_ROWS == 0
    tm = PROJ_TILE
    gains = norm_g.reshape(depth * norm_g.shape[1], 1, d)
    gain = lambda layer, i: _layer(gains, layer * norm_g.shape[1] + i)
    conv_in, conv_out = conv_in_w.astype(BF16), conv_out_w.astype(BF16)
    wq, wo = q_w.astype(BF16), o_w.astype(BF16)
    ffn_in, ffn_out = ffn_in_w.astype(BF16), ffn_out_w.astype(BF16)
    k_views = v_views = None
    for layer in range(depth):
        if layer < n_conv:
            x = _conv_mixer(x, [gain(layer, 0), gain(layer, 1), _layer(conv_in, layer),
                                _layer(conv_w, layer), _layer(conv_out, layer)], DENSE_TILE)
            x2 = x.reshape(b * s, d)
        else:
            j = layer - n_conv
            x2 = x.reshape(b * s, d)
            if k_views is None:
                q_views, k_views, v_views = _qkv_proj(
                    x2, [gain(layer, 0), _whole(kv_norm_g.reshape(1, d)), _layer(wq, j),
                         _whole(kv_w.astype(BF16))], tm)
            else:
                q_views = _q_proj(x2, [gain(layer, 0), _layer(wq, j)], tm)
            branches = [_attention_branch(q, k, v, dil, b)
                        for q, k, v, dil in zip(q_views, k_views, v_views, DILATIONS)]
            x2 = _merge_oproj(x2, [o for o, _ in branches], [st for _, st in branches],
                              [_layer(wo, j), gain(layer, 1)], tm)
        x2 = _ffn(x2, [gain(layer, 2), gain(layer, 3), _layer(ffn_in, layer), _layer(ffn_out, layer)],
                  DENSE_TILE)
        x = x2.reshape(b, s, d)
    return x
```
